```python
import math
import jax, jax.numpy as jnp
from jax import lax
import numpy as np

D_MODEL = 1024
BATCH = 32
SEQ = 2048
DEPTH = 4

N_A_LAYERS = DEPTH // 2
N_B_LAYERS = DEPTH - N_A_LAYERS
SSM_EXPAND = 2
SSM_D_INNER = SSM_EXPAND * D_MODEL
SSM_HEAD_DIM = 64
SSM_HEADS = SSM_D_INNER // SSM_HEAD_DIM
SSM_GROUPS = 4
SSM_HEADS_PER_GROUP = SSM_HEADS // SSM_GROUPS
SSM_STATE = 128
SSM_CONV = 4
SSM_CHUNK = 128
SSM_CONV_DIM = SSM_D_INNER + 2 * SSM_GROUPS * SSM_STATE
SSM_IN_DIM = 2 * SSM_D_INNER + 2 * SSM_GROUPS * SSM_STATE + SSM_HEADS
SB_HEADS = 16
SB_HEAD_DIM = 64
SB_WIDTH = SB_HEADS * SB_HEAD_DIM
SB_BLOCK = 128
D_FF = 2816
FFN_CONV = 3
PLE_DIM = 256
NORM_EPS = 1e-6
SSM_NORM_EPS = 1e-5

kernel_name = "yoco_mamba2_stickbreak_hybrid"


def rms_norm(x, gain, eps=NORM_EPS):
    xf = x.astype(jnp.float32)
    y = xf * lax.rsqrt(jnp.mean(xf * xf, axis=-1, keepdims=True) + eps)
    return (y * gain.astype(jnp.float32)).astype(x.dtype)


def causal_depthwise_conv(u, w, b):
    k = w.shape[0]
    out = lax.conv_general_dilated(
        u, w[:, None, :].astype(u.dtype), window_strides=(1,), padding=[(k - 1, 0)],
        dimension_numbers=("NWC", "WIO", "NWC"), feature_group_count=u.shape[-1])
    return out + b.astype(u.dtype)


def ssd_chunked(x, dt, a, b_mat, c_mat):
    bsz, seq = x.shape[0], x.shape[1]
    nc, L = seq // SSM_CHUNK, SSM_CHUNK
    G, R, P, N = SSM_GROUPS, SSM_HEADS_PER_GROUP, SSM_HEAD_DIM, SSM_STATE
    xr = (x * dt[..., None]).reshape(bsz, nc, L, G, R, P)
    br = b_mat.reshape(bsz, nc, L, G, N)
    cr = c_mat.reshape(bsz, nc, L, G, N)
    a_dt = (dt * a).reshape(bsz, nc, L, G, R).transpose(0, 1, 3, 4, 2)
    a_cs = jnp.cumsum(a_dt, axis=-1)
    tri = jnp.tril(jnp.ones((L, L), dtype=bool))
    decay_in = jnp.exp(jnp.where(tri, a_cs[..., :, None] - a_cs[..., None, :], -jnp.inf))
    cb = jnp.einsum("bclgn,bcsgn->bcgls", cr, br)
    y_diag = jnp.einsum("bcgls,bcgrls,bcsgrp->bclgrp", cb, decay_in, xr)
    decay_states = jnp.exp(a_cs[..., -1:] - a_cs)
    states = jnp.einsum("bclgn,bcgrl,bclgrp->bcgrpn", br, decay_states, xr)
    states = jnp.concatenate([jnp.zeros_like(states[:, :1]), states], axis=1)
    chunk_sum = jnp.pad(a_cs[..., -1].transpose(0, 2, 3, 1), ((0, 0), (0, 0), (0, 0), (1, 0)))
    chunk_cs = jnp.cumsum(chunk_sum, axis=-1)
    tri_c = jnp.tril(jnp.ones((nc + 1, nc + 1), dtype=bool))
    decay_chunk = jnp.exp(jnp.where(tri_c, chunk_cs[..., :, None] - chunk_cs[..., None, :], -jnp.inf))
    new_states = jnp.einsum("bgrzc,bcgrpn->bzgrpn", decay_chunk, states)
    prev_states = new_states[:, :-1]
    y_off = jnp.einsum("bclgn,bcgrpn,bcgrl->bclgrp", cr, prev_states, jnp.exp(a_cs))
    return (y_diag + y_off).reshape(bsz, seq, SSM_HEADS, P)


def mamba2_mixer(h, in_proj, conv_w, conv_b, dt_bias, a_log, d_skip, norm_w, out_proj):
    bsz, seq, _ = h.shape
    zxbcdt = h @ in_proj
    z = zxbcdt[..., :SSM_D_INNER]
    xbc = zxbcdt[..., SSM_D_INNER:SSM_D_INNER + SSM_CONV_DIM]
    dt_raw = zxbcdt[..., SSM_D_INNER + SSM_CONV_DIM:]
    xbc = jax.nn.silu(causal_depthwise_conv(xbc, conv_w, conv_b)).astype(jnp.float32)
    xs = xbc[..., :SSM_D_INNER].reshape(bsz, seq, SSM_HEADS, SSM_HEAD_DIM)
    bm = xbc[..., SSM_D_INNER:SSM_D_INNER + SSM_GROUPS * SSM_STATE].reshape(bsz, seq, SSM_GROUPS, SSM_STATE)
    cm = xbc[..., SSM_D_INNER + SSM_GROUPS * SSM_STATE:].reshape(bsz, seq, SSM_GROUPS, SSM_STATE)
    dt = jax.nn.softplus(dt_raw.astype(jnp.float32) + dt_bias.astype(jnp.float32))
    a = -jnp.exp(a_log.astype(jnp.float32))
    y = ssd_chunked(xs, dt, a, bm, cm) + xs * d_skip.astype(jnp.float32)[:, None]
    g = (y.reshape(bsz, seq, SSM_D_INNER) * jax.nn.silu(z.astype(jnp.float32))).reshape(bsz, seq, SSM_GROUPS, -1)
    g = g * lax.rsqrt(jnp.mean(g * g, axis=-1, keepdims=True) + SSM_NORM_EPS)
    g = g.reshape(bsz, seq, SSM_D_INNER) * norm_w.astype(jnp.float32)
    return g.astype(h.dtype) @ out_proj


def stick_breaking_attention(q, k, v):
    seq = q.shape[1]
    scale = SB_HEAD_DIM ** -0.5
    outs = []
    for blk in range(seq // SB_BLOCK):
        q0 = blk * SB_BLOCK
        q1 = q0 + SB_BLOCK
        z = jnp.einsum("bthd,bshd->bhts", q[:, q0:q1], k[:, :q1]).astype(jnp.float32) * scale
        causal = (q0 + jnp.arange(SB_BLOCK))[:, None] > jnp.arange(q1)[None, :]
        log_beta = jax.nn.log_sigmoid(z)
        log_keep = jnp.where(causal, jax.nn.log_sigmoid(-z), 0.0)
        log_rest = lax.cumsum(log_keep, axis=3, reverse=True) - log_keep
        w = jnp.where(causal, jnp.exp(log_beta + log_rest), 0.0)
        outs.append(jnp.einsum("bhts,bshd->bthd", w.astype(v.dtype), v[:, :q1]))
    return jnp.concatenate(outs, axis=1)


def conv_glu_ffn(h, w_up, conv_w, conv_b, w_down):
    u = causal_depthwise_conv(h @ w_up, conv_w, conv_b)
    gate, val = jnp.split(u, 2, axis=-1)
    return (jax.nn.silu(gate) * val) @ w_down


def per_layer_embedding(h, p_i, norm_gain, w_gate, w_proj):
    gate = jax.nn.sigmoid(rms_norm(h, norm_gain) @ w_gate)
    return gate * (p_i @ w_proj)


def setup_inputs(seed: int = 0) -> dict:
    key = jax.random.key(seed)
    ks = iter(jax.random.split(key, 32))

    def normal(shape, scale):
        return scale * jax.random.normal(next(ks), shape, jnp.float32)

    def gain(shape):
        return 1.0 + normal(shape, 0.02)

    x = normal((BATCH, SEQ, D_MODEL), 1.0)
    p = normal((DEPTH, BATCH, SEQ, PLE_DIM), 1.0)
    attn_norm = gain((DEPTH, D_MODEL))
    ffn_norm = gain((DEPTH, D_MODEL))
    ple_norm = gain((DEPTH, D_MODEL))
    ssm_in_proj = normal((N_A_LAYERS, D_MODEL, SSM_IN_DIM), D_MODEL ** -0.5)
    ssm_conv_w = normal((N_A_LAYERS, SSM_CONV, SSM_CONV_DIM), SSM_CONV ** -0.5)
    ssm_conv_b = normal((N_A_LAYERS, SSM_CONV_DIM), 0.01)
    u = jax.random.uniform(next(ks), (N_A_LAYERS, SSM_HEADS), jnp.float32)
    dt0 = jnp.exp(u * (math.log(0.1) - math.log(0.001)) + math.log(0.001))
    ssm_dt_bias = dt0 + jnp.log(-jnp.expm1(-dt0))
    ssm_a_log = jnp.log(jax.random.uniform(next(ks), (N_A_LAYERS, SSM_HEADS), jnp.float32, 1.0, 16.0))
    ssm_d = 1.0 + normal((N_A_LAYERS, SSM_HEADS), 0.1)
    ssm_norm = gain((N_A_LAYERS, SSM_D_INNER))
    ssm_out_proj = normal((N_A_LAYERS, SSM_D_INNER, D_MODEL), SSM_D_INNER ** -0.5)
    kv_norm = gain((D_MODEL,))
    w_kv = normal((D_MODEL, 2 * SB_WIDTH), D_MODEL ** -0.5)
    w_q = normal((N_B_LAYERS, D_MODEL, SB_WIDTH), D_MODEL ** -0.5)
    w_o = normal((N_B_LAYERS, SB_WIDTH, D_MODEL), SB_WIDTH ** -0.5)
    ffn_up = normal((DEPTH, D_MODEL, 2 * D_FF), D_MODEL ** -0.5)
    ffn_conv_w = normal((DEPTH, FFN_CONV, 2 * D_FF), FFN_CONV ** -0.5)
    ffn_conv_b = normal((DEPTH, 2 * D_FF), 0.01)
    ffn_down = normal((DEPTH, D_FF, D_MODEL), D_FF ** -0.5)
    ple_gate = normal((DEPTH, D_MODEL, D_MODEL), D_MODEL ** -0.5)
    ple_proj = normal((DEPTH, PLE_DIM, D_MODEL), PLE_DIM ** -0.5)
    final_norm = gain((D_MODEL,))
    return {"x": x, "p": p, "attn_norm": attn_norm, "ffn_norm": ffn_norm, "ple_norm": ple_norm,
            "ssm_in_proj": ssm_in_proj, "ssm_conv_w": ssm_conv_w, "ssm_conv_b": ssm_conv_b,
            "ssm_dt_bias": ssm_dt_bias, "ssm_a_log": ssm_a_log, "ssm_d": ssm_d,
            "ssm_norm": ssm_norm, "ssm_out_proj": ssm_out_proj, "kv_norm": kv_norm, "w_kv": w_kv,
            "w_q": w_q, "w_o": w_o, "ffn_up": ffn_up, "ffn_conv_w": ffn_conv_w,
            "ffn_conv_b": ffn_conv_b, "ffn_down": ffn_down, "ple_gate": ple_gate,
            "ple_proj": ple_proj, "final_norm": final_norm}


def reference(x, p, attn_norm, ffn_norm, ple_norm, ssm_in_proj, ssm_conv_w, ssm_conv_b,
              ssm_dt_bias, ssm_a_log, ssm_d, ssm_norm, ssm_out_proj, kv_norm, w_kv,
              w_q, w_o, ffn_up, ffn_conv_w, ffn_conv_b, ffn_down, ple_gate, ple_proj,
              final_norm):
    bsz, seq, _ = x.shape
    h = x
    k_shared = None
    v_shared = None
    for i in range(DEPTH):
        hn = rms_norm(h, attn_norm[i])
        if i < N_A_LAYERS:
            mix = mamba2_mixer(hn, ssm_in_proj[i], ssm_conv_w[i], ssm_conv_b[i], ssm_dt_bias[i],
                               ssm_a_log[i], ssm_d[i], ssm_norm[i], ssm_out_proj[i])
        else:
            j = i - N_A_LAYERS
            q = (hn @ w_q[j]).reshape(bsz, seq, SB_HEADS, SB_HEAD_DIM)
            o = stick_breaking_attention(q, k_shared, v_shared)
            mix = o.reshape(bsz, seq, SB_WIDTH) @ w_o[j]
        h = h + mix
        h = h + conv_glu_ffn(rms_norm(h, ffn_norm[i]), ffn_up[i], ffn_conv_w[i], ffn_conv_b[i], ffn_down[i])
        h = h + per_layer_embedding(h, p[i], ple_norm[i], ple_gate[i], ple_proj[i])
        if i == N_A_LAYERS - 1:
            kv = rms_norm(h, kv_norm) @ w_kv
            k_shared = kv[..., :SB_WIDTH].reshape(bsz, seq, SB_HEADS, SB_HEAD_DIM)
            v_shared = kv[..., SB_WIDTH:].reshape(bsz, seq, SB_HEADS, SB_HEAD_DIM)
    return rms_norm(h, final_norm)
```

```python
import functools
import math

import jax
import jax.numpy as jnp
from jax import lax
from jax.experimental import pallas as pl
from jax.experimental.pallas import tpu as pltpu

F32 = jnp.float32
BF16 = jnp.bfloat16

NORM_EPS = 1e-6
SSM_NORM_EPS = 1e-5
SSM_HEAD_DIM = 64
SSM_GROUPS = 4
SSM_STATE = 128
SSM_CHUNK = 128
SB_HEAD_DIM = 64
LANES = 128
SB_TILE = 256
ROW_TILE = 512
FF_CHUNK = 256
CONV_PAD = 8
VMEM_LIMIT = 56 * 1024 * 1024
LOG2E = 1.4426950408889634


def _params(*sem):
    return pltpu.CompilerParams(dimension_semantics=sem, vmem_limit_bytes=VMEM_LIMIT)


def _const_spec(shape):
    nd = len(shape)
    return pl.BlockSpec(shape, lambda *_: (0,) * nd, pipeline_mode=pl.Buffered(1))


def _rms_norm(x, gain, eps):
    return x * lax.rsqrt(jnp.mean(x * x, axis=-1, keepdims=True) + eps) * gain


def _sigmoid(x):
    return 1.0 / (1.0 + jnp.exp(-x))


def _softplus(x):
    return jnp.maximum(x, 0.0) + jnp.log1p(jnp.exp(-jnp.abs(x)))


def _dot(a, b):
    return jnp.dot(a, b, preferred_element_type=F32)


def _dot_nt(a, b):
    return lax.dot_general(a, b, (((1,), (1,)), ((), ())), preferred_element_type=F32)


def _split3(x):
    hi = x.astype(BF16)
    r1 = x - hi.astype(F32)
    mid = r1.astype(BF16)
    lo = (r1 - mid.astype(F32)).astype(BF16)
    return hi, mid, lo


def _norm_proj_kernel(h_ref, g_ref, *refs, transposed, eps):
    n = len(transposed)
    w_refs, o_refs = refs[:n], refs[n:]
    hn = _rms_norm(h_ref[...], g_ref[...], eps).astype(BF16)
    for w_ref, o_ref, tr in zip(w_refs, o_refs, transposed):
        if tr:
            o_ref[...] = _dot_nt(w_ref[...], hn).astype(o_ref.dtype)
        else:
            o_ref[...] = _dot(hn, w_ref[...]).astype(o_ref.dtype)


def _norm_proj(h, gain, weights, transposed, out_dtypes):
    bsz, seq, d = h.shape
    tm = min(ROW_TILE, seq)
    in_specs = [pl.BlockSpec((None, tm, d), lambda b, i: (b, i, 0)), _const_spec((1, d))]
    out_specs, out_shape = [], []
    for w, tr, dt in zip(weights, transposed, out_dtypes):
        in_specs.append(_const_spec(w.shape))
        if tr:
            n = w.shape[0]
            out_specs.append(pl.BlockSpec((None, n, tm), lambda b, i: (b, 0, i)))
            out_shape.append(jax.ShapeDtypeStruct((bsz, n, seq), dt))
        else:
            n = w.shape[1]
            out_specs.append(pl.BlockSpec((None, tm, n), lambda b, i: (b, i, 0)))
            out_shape.append(jax.ShapeDtypeStruct((bsz, seq, n), dt))
    return pl.pallas_call(
        functools.partial(_norm_proj_kernel, transposed=tuple(transposed), eps=NORM_EPS),
        grid=(bsz, seq // tm),
        in_specs=in_specs, out_specs=out_specs, out_shape=out_shape,
        compiler_params=_params("parallel", "parallel"),
        name="norm_proj",
    )(h, gain.reshape(1, d), *weights)


def _ssd_kernel(z_ref, xbc_ref, dt_ref, h_ref, cw_ref, cb_ref, dtb_ref, alog_ref, dexp_ref, nw_ref,
                wo_ref, o_ref, convbuf, xc, state, ybuf, *, d_inner, n_conv):
    L, N, G = SSM_CHUNK, SSM_STATE, SSM_GROUPS
    gw = d_inner // G
    pairs_per_group = gw // LANES
    conv_dim = d_inner + 2 * G * N
    c = pl.program_id(1)

    @pl.when(c == 0)
    def _():
        convbuf[0:CONV_PAD, :] = jnp.zeros((CONV_PAD, conv_dim), F32)
        state[...] = jnp.zeros(state.shape, F32)

    convbuf[CONV_PAD:CONV_PAD + L, :] = xbc_ref[...]
    for s0 in range(0, conv_dim, 512):
        sl = slice(s0, s0 + 512)
        acc = cb_ref[:, sl] + cw_ref[n_conv - 1:n_conv, sl] * convbuf[CONV_PAD:CONV_PAD + L, sl]
        for k in range(n_conv - 1):
            off = CONV_PAD - (n_conv - 1) + k
            acc = acc + cw_ref[k:k + 1, sl] * convbuf[off:off + L, sl]
        xc[:, sl] = acc * _sigmoid(acc)
    convbuf[0:CONV_PAD, :] = convbuf[L:L + CONV_PAD, :]

    dt = _softplus(dt_ref[...] + dtb_ref[...])
    adt = dt * (-jnp.exp(alog_ref[...]))
    row = lax.broadcasted_iota(jnp.int32, (L, L), 0)
    col = lax.broadcasted_iota(jnp.int32, (L, L), 1)
    tri = col <= row
    tri_b = jnp.where(tri, 1.0, 0.0).astype(BF16)
    a_cs = sum(_dot(tri_b, t) for t in _split3(adt))
    a_cs_t = a_cs.T
    dt_t = dt.T
    e_cs = jnp.exp(a_cs)
    w_t = jnp.exp(a_cs_t[:, L - 1:L] - a_cs_t) * dt_t
    a_last = a_cs[L - 1:L, :]
    lane = lax.broadcasted_iota(jnp.int32, (L, LANES), 1)
    first_head = lane < SSM_HEAD_DIM
    neg_inf = jnp.float32(-jnp.inf)

    for g in range(G):
        b_g = xc[:, d_inner + g * N:d_inner + (g + 1) * N]
        c_g = xc[:, d_inner + G * N + g * N:d_inner + G * N + (g + 1) * N]
        cb = _dot_nt(c_g.astype(BF16), b_g.astype(BF16))
        b_g_t = b_g.T
        for j in range(pairs_per_group):
            h0 = (g * pairs_per_group + j) * 2
            cols = slice((g * pairs_per_group + j) * LANES, (g * pairs_per_group + j + 1) * LANES)
            x_pair = xc[:, cols].astype(BF16)
            s_pair = state[g, :, j * LANES:(j + 1) * LANES]
            rhs = jnp.concatenate([x_pair, s_pair.astype(BF16)], axis=0)
            ys, sts = [], []
            for hh in (h0, h0 + 1):
                diff = a_cs[:, hh:hh + 1] - a_cs_t[hh:hh + 1, :]
                l_h = cb * jnp.exp(jnp.where(tri, diff, neg_inf)) * dt_t[hh:hh + 1, :]
                c_h = c_g * e_cs[:, hh:hh + 1]
                lhs = jnp.concatenate([l_h.astype(BF16), c_h.astype(BF16)], axis=1)
                ys.append(_dot(lhs, rhs))
                sts.append(_dot((b_g_t * w_t[hh:hh + 1, :]).astype(BF16), x_pair))
            ybuf[:, cols] = jnp.where(first_head, ys[0], ys[1])
            d_a = jnp.where(first_head[0:1, :], jnp.exp(a_last[:, h0:h0 + 1]), jnp.exp(a_last[:, h0 + 1:h0 + 2]))
            state[g, :, j * LANES:(j + 1) * LANES] = s_pair * d_a + jnp.where(first_head, sts[0], sts[1])

    acc = h_ref[...]
    for g in range(G):
        sl = slice(g * gw, (g + 1) * gw)
        zz = z_ref[:, sl]
        gated = (ybuf[:, sl] + xc[:, sl] * dexp_ref[:, sl]) * (zz * _sigmoid(zz))
        gn = gated * lax.rsqrt(jnp.mean(gated * gated, axis=-1, keepdims=True) + SSM_NORM_EPS) * nw_ref[:, sl]
        acc = acc + _dot(gn.astype(BF16), wo_ref[sl, :])
    o_ref[...] = acc


def _ssd_layer(h, z, xbc, dt, conv_w, conv_b, dt_bias, a_log, d_skip, norm_w, out_proj):
    bsz, seq, d = h.shape
    d_inner = z.shape[-1]
    conv_dim = xbc.shape[-1]
    n_conv = conv_w.shape[0]
    L = SSM_CHUNK
    pad = LANES - dt_bias.shape[0]
    row = lambda b, c: (b, c, 0)
    return pl.pallas_call(
        functools.partial(_ssd_kernel, d_inner=d_inner, n_conv=n_conv),
        grid=(bsz, seq // L),
        in_specs=[
            pl.BlockSpec((None, L, d_inner), row),
            pl.BlockSpec((None, L, conv_dim), row),
            pl.BlockSpec((None, L, LANES), row),
            pl.BlockSpec((None, L, d), row),
            _const_spec((n_conv, conv_dim)), _const_spec((1, conv_dim)),
            _const_spec((1, LANES)), _const_spec((1, LANES)),
            _const_spec((1, d_inner)), _const_spec((1, d_inner)),
            _const_spec((d_inner, d)),
        ],
        out_specs=pl.BlockSpec((None, L, d), row),
        out_shape=jax.ShapeDtypeStruct((bsz, seq, d), F32),
        scratch_shapes=[
            pltpu.VMEM((L + CONV_PAD, conv_dim), F32),
            pltpu.VMEM((L, conv_dim), F32),
            pltpu.VMEM((SSM_GROUPS, SSM_STATE, d_inner // SSM_GROUPS), F32),
            pltpu.VMEM((L, d_inner), F32),
        ],
        compiler_params=_params("parallel", "arbitrary"),
        name="ssd_layer",
    )(z, xbc, dt, h, conv_w, conv_b.reshape(1, conv_dim),
      jnp.pad(dt_bias, (0, pad)).reshape(1, LANES), jnp.pad(a_log, (0, pad)).reshape(1, LANES),
      jnp.repeat(d_skip, SSM_HEAD_DIM).reshape(1, d_inner), norm_w.reshape(1, d_inner),
      out_proj.astype(BF16))


def _sb_attn_kernel(q_ref, k_ref, vt_ref, o_ref):
    T, DH = SB_TILE, SB_HEAD_DIM
    qi = pl.program_id(2)
    lane = lax.broadcasted_iota(jnp.int32, (T, LANES), 1)
    q2 = q_ref[...]
    q_heads = (jnp.where(lane < DH, q2, jnp.zeros_like(q2)), jnp.where(lane < DH, jnp.zeros_like(q2), q2))
    row = lax.broadcasted_iota(jnp.int32, (T, T), 0)
    col = lax.broadcasted_iota(jnp.int32, (T, T), 1)
    tri_b = jnp.where(col >= row, 1.0, 0.0).astype(BF16)
    causal = row < col

    def block(kb, carry, diag):
        ks = pl.multiple_of(kb * T, T)
        k_blk = k_ref[pl.ds(ks, T), :]
        vt_blk = vt_ref[:, pl.ds(ks, T)]
        out = []
        for hh in range(2):
            run, acc = carry[hh]
            z = _dot_nt(k_blk, q_heads[hh])
            sp = jnp.maximum(jnp.log2(1.0 + jnp.exp2(jnp.minimum(z, 40.0))), z)
            if diag:
                sp = jnp.where(causal, sp, 0.0)
            hi = sp.astype(BF16)
            lo = (sp - hi.astype(F32)).astype(BF16)
            rest = _dot(tri_b, hi) + _dot(tri_b, lo)
            w = jnp.exp2(z - rest - run)
            if diag:
                w = jnp.where(causal, w, 0.0)
            acc = acc + _dot(vt_blk[hh * DH:(hh + 1) * DH, :], w.astype(BF16))
            out.append((run + rest[0:1, :], acc))
        return tuple(out)

    zero = (jnp.zeros((1, T), F32), jnp.zeros((DH, T), F32))
    carry = block(qi, (zero, zero), True)
    carry = lax.fori_loop(0, qi, lambda i, c: block(qi - 1 - i, c, False), carry)
    o_t = jnp.concatenate([carry[0][1], carry[1][1]], axis=0)
    o_ref[...] = o_t.T.astype(o_ref.dtype)


def _sb_attention(q, k, v_t):
    bsz, seq, width = q.shape
    T = SB_TILE
    return pl.pallas_call(
        _sb_attn_kernel,
        grid=(bsz, width // LANES, seq // T),
        in_specs=[
            pl.BlockSpec((None, T, LANES), lambda b, p, i: (b, i, p)),
            pl.BlockSpec((None, seq, LANES), lambda b, p, i: (b, 0, p)),
            pl.BlockSpec((None, LANES, seq), lambda b, p, i: (b, p, 0)),
        ],
        out_specs=pl.BlockSpec((None, T, LANES), lambda b, p, i: (b, i, p)),
        out_shape=jax.ShapeDtypeStruct((bsz, seq, width), BF16),
        compiler_params=_params("parallel", "parallel", "arbitrary"),
        name="sb_attention",
    )(q, k, v_t)


def _ffn_kernel(*refs, has_mix, final, n_conv):
    refs = list(refs)
    h_ref = refs.pop(0)
    if has_mix:
        mix_ref, wmix_ref = refs.pop(0), refs.pop(0)
    (p_ref, fg_ref, wg_ref, wv_ref, cwg_ref, cwv_ref, cbg_ref, cbv_ref, wd_ref,
     pg_ref, wpg_ref, wpp_ref) = refs[:12]
    refs = refs[12:]
    if final:
        fin_ref = refs.pop(0)
    o_ref, buf_g, buf_v, tail_g, tail_v, acc_ref, hn_ref = refs
    tm = h_ref.shape[0]
    n_chunks = wg_ref.shape[0]
    i = pl.program_id(1)

    @pl.when(i == 0)
    def _():
        tail_g[...] = jnp.zeros(tail_g.shape, F32)
        tail_v[...] = jnp.zeros(tail_v.shape, F32)

    h = h_ref[...]
    if has_mix:
        h = h + _dot(mix_ref[...], wmix_ref[...])
    acc_ref[...] = h
    hn_ref[...] = _rms_norm(h, fg_ref[...], NORM_EPS).astype(BF16)

    def conv(u, buf, tail, cw, cb, ci):
        buf[0:CONV_PAD, :] = tail[ci]
        buf[CONV_PAD:CONV_PAD + tm, :] = u
        tail[ci] = u[tm - CONV_PAD:tm, :]
        w = cw[ci]
        out = cb[ci] + w[n_conv - 1:n_conv, :] * u
        for k in range(n_conv - 1):
            off = CONV_PAD - (n_conv - 1) + k
            out = out + w[k:k + 1, :] * buf[off:off + tm, :]
        return out

    def chunk(ci, _):
        hn = hn_ref[...]
        gate = conv(_dot(hn, wg_ref[ci]), buf_g, tail_g, cwg_ref, cbg_ref, ci)
        val = conv(_dot(hn, wv_ref[ci]), buf_v, tail_v, cwv_ref, cbv_ref, ci)
        act = (gate * _sigmoid(gate)) * val
        acc_ref[...] += _dot(act.astype(BF16), wd_ref[ci])
        return 0

    lax.fori_loop(0, n_chunks, chunk, 0)

    h2 = acc_ref[...]
    gate = _sigmoid(_dot(_rms_norm(h2, pg_ref[...], NORM_EPS).astype(BF16), wpg_ref[...]))
    h3 = h2 + gate * _dot(p_ref[...].astype(BF16), wpp_ref[...])
    if final:
        h3 = _rms_norm(h3, fin_ref[...], NORM_EPS)
    o_ref[...] = h3


def _ffn_layer(h, p, layer, ffn_norm, w_up, conv_w, conv_b, w_down, ple_norm, ple_gate, ple_proj,
               mix=None, w_mix=None, final_norm=None):
    bsz, seq, d = h.shape
    d_ff = w_down.shape[0]
    n_conv = conv_w.shape[0]
    ple_dim = p.shape[-1]
    tm = min(ROW_TILE, seq)
    fc = FF_CHUNK
    n_chunks = d_ff // fc
    assert n_chunks * fc == d_ff

    def chunked_cols(w):
        return w.reshape(w.shape[0], n_chunks, fc).transpose(1, 0, 2)

    row = lambda b, i: (b, i, 0)
    args = [h]
    in_specs = [pl.BlockSpec((None, tm, d), row)]
    if mix is not None:
        args += [mix, w_mix.astype(BF16)]
        in_specs += [pl.BlockSpec((None, tm, mix.shape[-1]), row), _const_spec(w_mix.shape)]
    args += [
        p, ffn_norm.reshape(1, d),
        chunked_cols(w_up[:, :d_ff]).astype(BF16), chunked_cols(w_up[:, d_ff:]).astype(BF16),
        chunked_cols(conv_w[:, :d_ff]), chunked_cols(conv_w[:, d_ff:]),
        chunked_cols(conv_b[None, :d_ff]), chunked_cols(conv_b[None, d_ff:]),
        w_down.reshape(n_chunks, fc, d).astype(BF16),
        ple_norm.reshape(1, d), ple_gate.astype(BF16), ple_proj.astype(BF16),
    ]
    in_specs += [
        pl.BlockSpec((None, None, tm, ple_dim), lambda b, i: (layer, b, i, 0)),
        _const_spec((1, d)),
        _const_spec((n_chunks, d, fc)), _const_spec((n_chunks, d, fc)),
        _const_spec((n_chunks, n_conv, fc)), _const_spec((n_chunks, n_conv, fc)),
        _const_spec((n_chunks, 1, fc)), _const_spec((n_chunks, 1, fc)),
        _const_spec((n_chunks, fc, d)),
        _const_spec((1, d)), _const_spec((d, d)), _const_spec((ple_dim, d)),
    ]
    if final_norm is not None:
        args.append(final_norm.reshape(1, d))
        in_specs.append(_const_spec((1, d)))
    return pl.pallas_call(
        functools.partial(_ffn_kernel, has_mix=mix is not None, final=final_norm is not None, n_conv=n_conv),
        grid=(bsz, seq // tm),
        in_specs=in_specs,
        out_specs=pl.BlockSpec((None, tm, d), row),
        out_shape=jax.ShapeDtypeStruct((bsz, seq, d), F32),
        scratch_shapes=[
            pltpu.VMEM((tm + CONV_PAD, fc), F32), pltpu.VMEM((tm + CONV_PAD, fc), F32),
            pltpu.VMEM((n_chunks, CONV_PAD, fc), F32), pltpu.VMEM((n_chunks, CONV_PAD, fc), F32),
            pltpu.VMEM((tm, d), F32),
            pltpu.VMEM((tm, d), BF16),
        ],
        compiler_params=_params("parallel", "arbitrary"),
        name="ffn_layer",
    )(*args)


def kernel(x, p, attn_norm, ffn_norm, ple_norm, ssm_in_proj, ssm_conv_w, ssm_conv_b, ssm_dt_bias, ssm_a_log, ssm_d, ssm_norm, ssm_out_proj, kv_norm, w_kv, w_q, w_o, ffn_up, ffn_conv_w, ffn_conv_b, ffn_down, ple_gate, ple_proj, final_norm):
    depth = attn_norm.shape[0]
    n_a = ssm_in_proj.shape[0]
    d_inner = ssm_out_proj.shape[1]
    conv_dim = ssm_conv_w.shape[-1]
    n_heads = ssm_dt_bias.shape[-1]
    sb_width = w_q.shape[-1]
    q_scale = (SB_HEAD_DIM ** -0.5) * LOG2E

    h = x
    k = v_t = None
    for i in range(depth):
        ffn_args = (ffn_norm[i], ffn_up[i], ffn_conv_w[i], ffn_conv_b[i], ffn_down[i],
                    ple_norm[i], ple_gate[i], ple_proj[i])
        fin = final_norm if i == depth - 1 else None
        if i < n_a:
            w_in = ssm_in_proj[i]
            w_z = w_in[:, :d_inner].astype(BF16)
            w_xbc = w_in[:, d_inner:d_inner + conv_dim].astype(BF16)
            w_dt = jnp.pad(w_in[:, d_inner + conv_dim:], ((0, 0), (0, LANES - n_heads))).astype(BF16)
            z, xbc, dt = _norm_proj(h, attn_norm[i], [w_z, w_xbc, w_dt], [False] * 3, [F32] * 3)
            h = _ssd_layer(h, z, xbc, dt, ssm_conv_w[i], ssm_conv_b[i], ssm_dt_bias[i], ssm_a_log[i],
                           ssm_d[i], ssm_norm[i], ssm_out_proj[i])
            h = _ffn_layer(h, p, i, *ffn_args, final_norm=fin)
        else:
            j = i - n_a
            (q,) = _norm_proj(h, attn_norm[i], [(w_q[j] * q_scale).astype(BF16)], [False], [BF16])
            o = _sb_attention(q, k, v_t)
            h = _ffn_layer(h, p, i, *ffn_args, mix=o, w_mix=w_o[j], final_norm=fin)
        if i == n_a - 1:
            w_k = w_kv[:, :sb_width].astype(BF16)
            w_v_t = w_kv[:, sb_width:].T.astype(BF16)
            k, v_t = _norm_proj(h, kv_norm, [w_k, w_v_t], [False, True], [BF16, BF16])
    return h
```

```python
import functools
import math

import jax
import jax.numpy as jnp
from jax import lax
from jax.experimental import pallas as pl
from jax.experimental.pallas import tpu as pltpu

F32 = jnp.float32
BF16 = jnp.bfloat16

NORM_EPS = 1e-6
SSM_NORM_EPS = 1e-5
SSM_HEAD_DIM = 64
SSM_GROUPS = 4
SSM_STATE = 128
SSM_CHUNK = 128
SB_HEAD_DIM = 64
LANES = 128
SB_TILE = 256
SB_STEP_HEADS = 4
ROW_TILE = 512
FF_CHUNK = 256
CONV_PAD = 8
VMEM_LIMIT = 56 * 1024 * 1024
LOG2E = 1.4426950408889634


def _params(*sem):
    return pltpu.CompilerParams(dimension_semantics=sem, vmem_limit_bytes=VMEM_LIMIT)


def _const_spec(shape):
    nd = len(shape)
    return pl.BlockSpec(shape, lambda *_: (0,) * nd, pipeline_mode=pl.Buffered(1))


def _rms_norm(x, gain, eps):
    return x * lax.rsqrt(jnp.mean(x * x, axis=-1, keepdims=True) + eps) * gain


def _sigmoid(x):
    return 1.0 / (1.0 + jnp.exp(-x))


def _softplus(x):
    return jnp.maximum(x, 0.0) + jnp.log1p(jnp.exp(-jnp.abs(x)))


def _dot(a, b):
    return jnp.dot(a, b, preferred_element_type=F32)


def _dot_nt(a, b):
    return lax.dot_general(a, b, (((1,), (1,)), ((), ())), preferred_element_type=F32)


def _split3(x):
    hi = x.astype(BF16)
    r1 = x - hi.astype(F32)
    mid = r1.astype(BF16)
    lo = (r1 - mid.astype(F32)).astype(BF16)
    return hi, mid, lo


def _norm_proj_kernel(h_ref, g_ref, *refs, transposed, eps):
    n = len(transposed)
    w_refs, o_refs = refs[:n], refs[n:]
    hn = _rms_norm(h_ref[...], g_ref[...], eps).astype(BF16)
    for w_ref, o_ref, tr in zip(w_refs, o_refs, transposed):
        if tr:
            o_ref[...] = _dot_nt(w_ref[...], hn).astype(o_ref.dtype)
        else:
            o_ref[...] = _dot(hn, w_ref[...]).astype(o_ref.dtype)


def _norm_proj(h, gain, weights, transposed, out_dtypes):
    bsz, seq, d = h.shape
    tm = min(ROW_TILE, seq)
    in_specs = [pl.BlockSpec((None, tm, d), lambda b, i: (b, i, 0)), _const_spec((1, d))]
    out_specs, out_shape = [], []
    for w, tr, dt in zip(weights, transposed, out_dtypes):
        in_specs.append(_const_spec(w.shape))
        if tr:
            n = w.shape[0]
            out_specs.append(pl.BlockSpec((None, n, tm), lambda b, i: (b, 0, i)))
            out_shape.append(jax.ShapeDtypeStruct((bsz, n, seq), dt))
        else:
            n = w.shape[1]
            out_specs.append(pl.BlockSpec((None, tm, n), lambda b, i: (b, i, 0)))
            out_shape.append(jax.ShapeDtypeStruct((bsz, seq, n), dt))
    return pl.pallas_call(
        functools.partial(_norm_proj_kernel, transposed=tuple(transposed), eps=NORM_EPS),
        grid=(bsz, seq // tm),
        in_specs=in_specs, out_specs=out_specs, out_shape=out_shape,
        compiler_params=_params("parallel", "parallel"),
        name="norm_proj",
    )(h, gain.reshape(1, d), *weights)


def _ssd_kernel(z_ref, xbc_ref, dt_ref, h_ref, cw_ref, cb_ref, dtb_ref, alog_ref, dexp_ref, nw_ref,
                wo_ref, o_ref, convbuf, xc, state, ybuf, *, d_inner, n_conv):
    L, N, G = SSM_CHUNK, SSM_STATE, SSM_GROUPS
    gw = d_inner // G
    pairs_per_group = gw // LANES
    conv_dim = d_inner + 2 * G * N
    c = pl.program_id(1)

    @pl.when(c == 0)
    def _():
        convbuf[0:CONV_PAD, :] = jnp.zeros((CONV_PAD, conv_dim), F32)
        state[...] = jnp.zeros(state.shape, F32)

    convbuf[CONV_PAD:CONV_PAD + L, :] = xbc_ref[...]
    for s0 in range(0, conv_dim, 512):
        sl = slice(s0, s0 + 512)
        acc = cb_ref[:, sl] + cw_ref[n_conv - 1:n_conv, sl] * convbuf[CONV_PAD:CONV_PAD + L, sl]
        for k in range(n_conv - 1):
            off = CONV_PAD - (n_conv - 1) + k
            acc = acc + cw_ref[k:k + 1, sl] * convbuf[off:off + L, sl]
        xc[:, sl] = acc * _sigmoid(acc)
    convbuf[0:CONV_PAD, :] = convbuf[L:L + CONV_PAD, :]

    dt = _softplus(dt_ref[...] + dtb_ref[...])
    adt = dt * (-jnp.exp(alog_ref[...]))
    row = lax.broadcasted_iota(jnp.int32, (L, L), 0)
    col = lax.broadcasted_iota(jnp.int32, (L, L), 1)
    tri = col <= row
    tri_b = jnp.where(tri, 1.0, 0.0).astype(BF16)
    a_cs = sum(_dot(tri_b, t) for t in _split3(adt))
    a_cs_t = a_cs.T
    dt_t = dt.T
    e_cs = jnp.exp(a_cs)
    w_t = jnp.exp(a_cs_t[:, L - 1:L] - a_cs_t) * dt_t
    a_last = a_cs[L - 1:L, :]
    lane = lax.broadcasted_iota(jnp.int32, (L, LANES), 1)
    first_head = lane < SSM_HEAD_DIM
    neg_inf = jnp.float32(-jnp.inf)

    for g in range(G):
        b_g = xc[:, d_inner + g * N:d_inner + (g + 1) * N]
        c_g = xc[:, d_inner + G * N + g * N:d_inner + G * N + (g + 1) * N]
        cb = _dot_nt(c_g.astype(BF16), b_g.astype(BF16))
        b_g_t = b_g.T
        for j in range(pairs_per_group):
            h0 = (g * pairs_per_group + j) * 2
            cols = slice((g * pairs_per_group + j) * LANES, (g * pairs_per_group + j + 1) * LANES)
            x_pair = xc[:, cols].astype(BF16)
            s_pair = state[g, :, j * LANES:(j + 1) * LANES]
            rhs = jnp.concatenate([x_pair, s_pair.astype(BF16)], axis=0)
            ys, sts = [], []
            for hh in (h0, h0 + 1):
                diff = a_cs[:, hh:hh + 1] - a_cs_t[hh:hh + 1, :]
                l_h = cb * jnp.exp(jnp.where(tri, diff, neg_inf)) * dt_t[hh:hh + 1, :]
                c_h = c_g * e_cs[:, hh:hh + 1]
                lhs = jnp.concatenate([l_h.astype(BF16), c_h.astype(BF16)], axis=1)
                ys.append(_dot(lhs, rhs))
                sts.append(_dot((b_g_t * w_t[hh:hh + 1, :]).astype(BF16), x_pair))
            ybuf[:, cols] = jnp.where(first_head, ys[0], ys[1])
            d_a = jnp.where(first_head[0:1, :], jnp.exp(a_last[:, h0:h0 + 1]), jnp.exp(a_last[:, h0 + 1:h0 + 2]))
            state[g, :, j * LANES:(j + 1) * LANES] = s_pair * d_a + jnp.where(first_head, sts[0], sts[1])

    acc = h_ref[...]
    for g in range(G):
        sl = slice(g * gw, (g + 1) * gw)
        zz = z_ref[:, sl]
        gated = (ybuf[:, sl] + xc[:, sl] * dexp_ref[:, sl]) * (zz * _sigmoid(zz))
        gn = gated * lax.rsqrt(jnp.mean(gated * gated, axis=-1, keepdims=True) + SSM_NORM_EPS) * nw_ref[:, sl]
        acc = acc + _dot(gn.astype(BF16), wo_ref[sl, :])
    o_ref[...] = acc


def _ssd_layer(h, z, xbc, dt, conv_w, conv_b, dt_bias, a_log, d_skip, norm_w, out_proj):
    bsz, seq, d = h.shape
    d_inner = z.shape[-1]
    conv_dim = xbc.shape[-1]
    n_conv = conv_w.shape[0]
    L = SSM_CHUNK
    pad = LANES - dt_bias.shape[0]
    row = lambda b, c: (b, c, 0)
    return pl.pallas_call(
        functools.partial(_ssd_kernel, d_inner=d_inner, n_conv=n_conv),
        grid=(bsz, seq // L),
        in_specs=[
            pl.BlockSpec((None, L, d_inner), row),
            pl.BlockSpec((None, L, conv_dim), row),
            pl.BlockSpec((None, L, LANES), row),
            pl.BlockSpec((None, L, d), row),
            _const_spec((n_conv, conv_dim)), _const_spec((1, conv_dim)),
            _const_spec((1, LANES)), _const_spec((1, LANES)),
            _const_spec((1, d_inner)), _const_spec((1, d_inner)),
            _const_spec((d_inner, d)),
        ],
        out_specs=pl.BlockSpec((None, L, d), row),
        out_shape=jax.ShapeDtypeStruct((bsz, seq, d), F32),
        scratch_shapes=[
            pltpu.VMEM((L + CONV_PAD, conv_dim), F32),
            pltpu.VMEM((L, conv_dim), F32),
            pltpu.VMEM((SSM_GROUPS, SSM_STATE, d_inner // SSM_GROUPS), F32),
            pltpu.VMEM((L, d_inner), F32),
        ],
        compiler_params=_params("parallel", "arbitrary"),
        name="ssd_layer",
    )(z, xbc, dt, h, conv_w, conv_b.reshape(1, conv_dim),
      jnp.pad(dt_bias, (0, pad)).reshape(1, LANES), jnp.pad(a_log, (0, pad)).reshape(1, LANES),
      jnp.repeat(d_skip, SSM_HEAD_DIM).reshape(1, d_inner), norm_w.reshape(1, d_inner),
      out_proj.astype(BF16))


def _sb_attn_kernel(q_ref, k_ref, vt_ref, o_ref, run_ref, acc_ref):
    T, DH = SB_TILE, SB_HEAD_DIM
    n_heads = q_ref.shape[1] // DH
    qi = pl.program_id(2)
    lane = lax.broadcasted_iota(jnp.int32, (T, LANES), 1)
    q_heads = []
    for p in range(n_heads // 2):
        q2 = q_ref[:, p * LANES:(p + 1) * LANES]
        q_heads += [jnp.where(lane < DH, q2, jnp.zeros_like(q2)), jnp.where(lane < DH, jnp.zeros_like(q2), q2)]
    row = lax.broadcasted_iota(jnp.int32, (T, T), 0)
    col = lax.broadcasted_iota(jnp.int32, (T, T), 1)
    tri_b = jnp.where(col >= row, 1.0, 0.0).astype(BF16)
    causal = row < col
    heads = range(n_heads)

    def block(kb, diag):
        ks = pl.multiple_of(kb * T, T)
        zs = [_dot_nt(k_ref[pl.ds(ks, T), (h // 2) * LANES:(h // 2 + 1) * LANES], q_heads[h]) for h in heads]
        his, los = [], []
        for h in heads:
            z = zs[h]
            sp = jnp.maximum(jnp.log2(1.0 + jnp.exp2(jnp.minimum(z, 40.0))), z)
            if diag:
                sp = jnp.where(causal, sp, 0.0)
            hi = sp.astype(BF16)
            his.append(hi)
            los.append((sp - hi.astype(F32)).astype(BF16))
        rests = [_dot(tri_b, his[h]) + _dot(tri_b, los[h]) for h in heads]
        ws = []
        for h in heads:
            w = jnp.exp2(zs[h] - rests[h] - run_ref[h:h + 1, :])
            if diag:
                w = jnp.where(causal, w, 0.0)
            ws.append(w.astype(BF16))
            run_ref[h:h + 1, :] += rests[h][0:1, :]
        for h in heads:
            acc_ref[h * DH:(h + 1) * DH, :] += _dot(vt_ref[h * DH:(h + 1) * DH, pl.ds(ks, T)], ws[h])

    run_ref[...] = jnp.zeros(run_ref.shape, F32)
    acc_ref[...] = jnp.zeros(acc_ref.shape, F32)
    block(qi, True)

    def body(i, _):
        block(qi - 1 - i, False)
        return 0

    lax.fori_loop(0, qi, body, 0)
    o_ref[...] = acc_ref[...].T.astype(o_ref.dtype)


def _sb_attention(q, k, v_t):
    bsz, seq, width = q.shape
    T = SB_TILE
    W = SB_STEP_HEADS * SB_HEAD_DIM
    return pl.pallas_call(
        _sb_attn_kernel,
        grid=(bsz, width // W, seq // T),
        in_specs=[
            pl.BlockSpec((None, T, W), lambda b, p, i: (b, i, p)),
            pl.BlockSpec((None, seq, W), lambda b, p, i: (b, 0, p)),
            pl.BlockSpec((None, W, seq), lambda b, p, i: (b, p, 0)),
        ],
        out_specs=pl.BlockSpec((None, T, W), lambda b, p, i: (b, i, p)),
        out_shape=jax.ShapeDtypeStruct((bsz, seq, width), BF16),
        scratch_shapes=[
            pltpu.VMEM((8, T), F32),
            pltpu.VMEM((W, T), F32),
        ],
        compiler_params=_params("parallel", "parallel", "arbitrary"),
        name="sb_attention",
    )(q, k, v_t)


def _ffn_kernel(*refs, has_mix, final, n_conv):
    refs = list(refs)
    h_ref = refs.pop(0)
    if has_mix:
        mix_ref, wmix_ref = refs.pop(0), refs.pop(0)
    (p_ref, fg_ref, wg_ref, wv_ref, cwg_ref, cwv_ref, cbg_ref, cbv_ref, wd_ref,
     pg_ref, wpg_ref, wpp_ref) = refs[:12]
    refs = refs[12:]
    if final:
        fin_ref = refs.pop(0)
    o_ref, buf_g, buf_v, tail_g, tail_v, acc_ref, hn_ref = refs
    tm = h_ref.shape[0]
    n_chunks = wg_ref.shape[0]
    i = pl.program_id(1)

    @pl.when(i == 0)
    def _():
        tail_g[...] = jnp.zeros(tail_g.shape, F32)
        tail_v[...] = jnp.zeros(tail_v.shape, F32)

    h = h_ref[...]
    if has_mix:
        h = h + _dot(mix_ref[...], wmix_ref[...])
    acc_ref[...] = h
    hn_ref[...] = _rms_norm(h, fg_ref[...], NORM_EPS).astype(BF16)

    def conv(u, buf, tail, cw, cb, ci):
        buf[0:CONV_PAD, :] = tail[ci]
        buf[CONV_PAD:CONV_PAD + tm, :] = u
        tail[ci] = u[tm - CONV_PAD:tm, :]
        w = cw[ci]
        out = cb[ci] + w[n_conv - 1:n_conv, :] * u
        for k in range(n_conv - 1):
            off = CONV_PAD - (n_conv - 1) + k
            out = out + w[k:k + 1, :] * buf[off:off + tm, :]
        return out

    def up(ci):
        hn = hn_ref[...]
        return _dot(hn, wg_ref[ci]), _dot(hn, wv_ref[ci])

    u_next = up(0)
    for ci in range(n_chunks):
        u_g, u_v = u_next
        if ci + 1 < n_chunks:
            u_next = up(ci + 1)
        gate = conv(u_g, buf_g.at[ci % 2], tail_g, cwg_ref, cbg_ref, ci)
        val = conv(u_v, buf_v.at[ci % 2], tail_v, cwv_ref, cbv_ref, ci)
        act = (gate * _sigmoid(gate)) * val
        acc_ref[...] += _dot(act.astype(BF16), wd_ref[ci])

    h2 = acc_ref[...]
    gate = _sigmoid(_dot(_rms_norm(h2, pg_ref[...], NORM_EPS).astype(BF16), wpg_ref[...]))
    h3 = h2 + gate * _dot(p_ref[...].astype(BF16), wpp_ref[...])
    if final:
        h3 = _rms_norm(h3, fin_ref[...], NORM_EPS)
    o_ref[...] = h3


def _ffn_layer(h, p, layer, ffn_norm, w_up, conv_w, conv_b, w_down, ple_norm, ple_gate, ple_proj,
               mix=None, w_mix=None, final_norm=None):
    bsz, seq, d = h.shape
    d_ff = w_down.shape[0]
    n_conv = conv_w.shape[0]
    ple_dim = p.shape[-1]
    tm = min(ROW_TILE, seq)
    fc = FF_CHUNK
    n_chunks = d_ff // fc
    assert n_chunks * fc == d_ff

    def chunked_cols(w):
        return w.reshape(w.shape[0], n_chunks, fc).transpose(1, 0, 2)

    row = lambda b, i: (b, i, 0)
    args = [h]
    in_specs = [pl.BlockSpec((None, tm, d), row)]
    if mix is not None:
        args += [mix, w_mix.astype(BF16)]
        in_specs += [pl.BlockSpec((None, tm, mix.shape[-1]), row), _const_spec(w_mix.shape)]
    args += [
        p, ffn_norm.reshape(1, d),
        chunked_cols(w_up[:, :d_ff]).astype(BF16), chunked_cols(w_up[:, d_ff:]).astype(BF16),
        chunked_cols(conv_w[:, :d_ff]), chunked_cols(conv_w[:, d_ff:]),
        chunked_cols(conv_b[None, :d_ff]), chunked_cols(conv_b[None, d_ff:]),
        w_down.reshape(n_chunks, fc, d).astype(BF16),
        ple_norm.reshape(1, d), ple_gate.astype(BF16), ple_proj.astype(BF16),
    ]
    in_specs += [
        pl.BlockSpec((None, None, tm, ple_dim), lambda b, i: (layer, b, i, 0)),
        _const_spec((1, d)),
        _const_spec((n_chunks, d, fc)), _const_spec((n_chunks, d, fc)),
        _const_spec((n_chunks, n_conv, fc)), _const_spec((n_chunks, n_conv, fc)),
        _const_spec((n_chunks, 1, fc)), _const_spec((n_chunks, 1, fc)),
        _const_spec((n_chunks, fc, d)),
        _const_spec((1, d)), _const_spec((d, d)), _const_spec((ple_dim, d)),
    ]
    if final_norm is not None:
        args.append(final_norm.reshape(1, d))
        in_specs.append(_const_spec((1, d)))
    return pl.pallas_call(
        functools.partial(_ffn_kernel, has_mix=mix is not None, final=final_norm is not None, n_conv=n_conv),
        grid=(bsz, seq // tm),
        in_specs=in_specs,
        out_specs=pl.BlockSpec((None, tm, d), row),
        out_shape=jax.ShapeDtypeStruct((bsz, seq, d), F32),
        scratch_shapes=[
            pltpu.VMEM((2, tm + CONV_PAD, fc), F32), pltpu.VMEM((2, tm + CONV_PAD, fc), F32),
            pltpu.VMEM((n_chunks, CONV_PAD, fc), F32), pltpu.VMEM((n_chunks, CONV_PAD, fc), F32),
            pltpu.VMEM((tm, d), F32),
            pltpu.VMEM((tm, d), BF16),
        ],
        compiler_params=_params("parallel", "arbitrary"),
        name="ffn_layer",
    )(*args)


def kernel(x, p, attn_norm, ffn_norm, ple_norm, ssm_in_proj, ssm_conv_w, ssm_conv_b, ssm_dt_bias, ssm_a_log, ssm_d, ssm_norm, ssm_out_proj, kv_norm, w_kv, w_q, w_o, ffn_up, ffn_conv_w, ffn_conv_b, ffn_down, ple_gate, ple_proj, final_norm):
    depth = attn_norm.shape[0]
    n_a = ssm_in_proj.shape[0]
    d_inner = ssm_out_proj.shape[1]
    conv_dim = ssm_conv_w.shape[-1]
    n_heads = ssm_dt_bias.shape[-1]
    sb_width = w_q.shape[-1]
    q_scale = (SB_HEAD_DIM ** -0.5) * LOG2E

    h = x
    k = v_t = None
    for i in range(depth):
        ffn_args = (ffn_norm[i], ffn_up[i], ffn_conv_w[i], ffn_conv_b[i], ffn_down[i],
                    ple_norm[i], ple_gate[i], ple_proj[i])
        fin = final_norm if i == depth - 1 else None
        if i < n_a:
            w_in = ssm_in_proj[i]
            w_z = w_in[:, :d_inner].astype(BF16)
            w_xbc = w_in[:, d_inner:d_inner + conv_dim].astype(BF16)
            w_dt = jnp.pad(w_in[:, d_inner + conv_dim:], ((0, 0), (0, LANES - n_heads))).astype(BF16)
            z, xbc, dt = _norm_proj(h, attn_norm[i], [w_z, w_xbc, w_dt], [False] * 3, [F32] * 3)
            h = _ssd_layer(h, z, xbc, dt, ssm_conv_w[i], ssm_conv_b[i], ssm_dt_bias[i], ssm_a_log[i],
                           ssm_d[i], ssm_norm[i], ssm_out_proj[i])
            h = _ffn_layer(h, p, i, *ffn_args, final_norm=fin)
        else:
            j = i - n_a
            (q,) = _norm_proj(h, attn_norm[i], [(w_q[j] * q_scale).astype(BF16)], [False], [BF16])
            o = _sb_attention(q, k, v_t)
            h = _ffn_layer(h, p, i, *ffn_args, mix=o, w_mix=w_o[j], final_norm=fin)
        if i == n_a - 1:
            w_k = w_kv[:, :sb_width].astype(BF16)
            w_v_t = w_kv[:, sb_width:].T.astype(BF16)
            k, v_t = _norm_proj(h, kv_norm, [w_k, w_v_t], [False, True], [BF16, BF16])
    return h
```

```python
import functools
import math

import jax
import jax.numpy as jnp
from jax import lax
from jax.experimental import pallas as pl
from jax.experimental.pallas import tpu as pltpu

F32 = jnp.float32
BF16 = jnp.bfloat16

NORM_EPS = 1e-6
SSM_NORM_EPS = 1e-5
SSM_HEAD_DIM = 64
SSM_GROUPS = 4
SSM_STATE = 128
SSM_CHUNK = 128
SB_HEAD_DIM = 64
LANES = 128
SUBLANES = 8
SB_TILE = 256
SB_STEP_HEADS = 4
SB_UNDERFLOW_LOG2 = 160.0
ROW_TILE = 512
FF_CHUNK = 256
FF_LOOKAHEAD = 2
CONV_PAD = 8
VMEM_LIMIT = 56 * 1024 * 1024
LOG2E = 1.4426950408889634


def _params(*sem):
    return pltpu.CompilerParams(dimension_semantics=sem, vmem_limit_bytes=VMEM_LIMIT)


def _const_spec(shape):
    nd = len(shape)
    return pl.BlockSpec(shape, lambda *_: (0,) * nd, pipeline_mode=pl.Buffered(1))


def _rms_norm(x, gain, eps):
    return x * lax.rsqrt(jnp.mean(x * x, axis=-1, keepdims=True) + eps) * gain


def _sigmoid(x):
    return 1.0 / (1.0 + jnp.exp(-x))


def _softplus(x):
    return jnp.maximum(x, 0.0) + jnp.log1p(jnp.exp(-jnp.abs(x)))


def _dot(a, b):
    return jnp.dot(a, b, preferred_element_type=F32)


def _dot_nt(a, b):
    return lax.dot_general(a, b, (((1,), (1,)), ((), ())), preferred_element_type=F32)


def _split3(x):
    hi = x.astype(BF16)
    r1 = x - hi.astype(F32)
    mid = r1.astype(BF16)
    lo = (r1 - mid.astype(F32)).astype(BF16)
    return hi, mid, lo


def _norm_proj_kernel(h_ref, g_ref, *refs, transposed, eps):
    n = len(transposed)
    w_refs, o_refs = refs[:n], refs[n:]
    hn = _rms_norm(h_ref[...], g_ref[...], eps).astype(BF16)
    for w_ref, o_ref, tr in zip(w_refs, o_refs, transposed):
        if tr:
            o_ref[...] = _dot_nt(w_ref[...], hn).astype(o_ref.dtype)
        else:
            o_ref[...] = _dot(hn, w_ref[...]).astype(o_ref.dtype)


def _norm_proj(h, gain, weights, transposed, out_dtypes):
    bsz, seq, d = h.shape
    tm = min(ROW_TILE, seq)
    in_specs = [pl.BlockSpec((None, tm, d), lambda b, i: (b, i, 0)), _const_spec((1, d))]
    out_specs, out_shape = [], []
    for w, tr, dt in zip(weights, transposed, out_dtypes):
        in_specs.append(_const_spec(w.shape))
        if tr:
            n = w.shape[0]
            out_specs.append(pl.BlockSpec((None, n, tm), lambda b, i: (b, 0, i)))
            out_shape.append(jax.ShapeDtypeStruct((bsz, n, seq), dt))
        else:
            n = w.shape[1]
            out_specs.append(pl.BlockSpec((None, tm, n), lambda b, i: (b, i, 0)))
            out_shape.append(jax.ShapeDtypeStruct((bsz, seq, n), dt))
    return pl.pallas_call(
        functools.partial(_norm_proj_kernel, transposed=tuple(transposed), eps=NORM_EPS),
        grid=(bsz, seq // tm),
        in_specs=in_specs, out_specs=out_specs, out_shape=out_shape,
        compiler_params=_params("parallel", "parallel"),
        name="norm_proj",
    )(h, gain.reshape(1, d), *weights)


def _ssd_kernel(z_ref, xbc_ref, dt_ref, h_ref, cw_ref, cb_ref, dtb_ref, alog_ref, dexp_ref, nw_ref,
                wo_ref, o_ref, convbuf, xc, state, ybuf, *, d_inner, n_conv):
    L, N, G = SSM_CHUNK, SSM_STATE, SSM_GROUPS
    gw = d_inner // G
    pairs_per_group = gw // LANES
    conv_dim = d_inner + 2 * G * N
    c = pl.program_id(1)

    @pl.when(c == 0)
    def _():
        convbuf[0:CONV_PAD, :] = jnp.zeros((CONV_PAD, conv_dim), F32)
        state[...] = jnp.zeros(state.shape, F32)

    convbuf[CONV_PAD:CONV_PAD + L, :] = xbc_ref[...]
    for s0 in range(0, conv_dim, 512):
        sl = slice(s0, s0 + 512)
        acc = cb_ref[:, sl] + cw_ref[n_conv - 1:n_conv, sl] * convbuf[CONV_PAD:CONV_PAD + L, sl]
        for k in range(n_conv - 1):
            off = CONV_PAD - (n_conv - 1) + k
            acc = acc + cw_ref[k:k + 1, sl] * convbuf[off:off + L, sl]
        xc[:, sl] = acc * _sigmoid(acc)
    convbuf[0:CONV_PAD, :] = convbuf[L:L + CONV_PAD, :]

    dt = _softplus(dt_ref[...] + dtb_ref[...])
    adt = dt * (-jnp.exp(alog_ref[...]))
    row = lax.broadcasted_iota(jnp.int32, (L, L), 0)
    col = lax.broadcasted_iota(jnp.int32, (L, L), 1)
    tri = col <= row
    tri_b = jnp.where(tri, 1.0, 0.0).astype(BF16)
    a_cs = sum(_dot(tri_b, t) for t in _split3(adt))
    a_cs_t = a_cs.T
    dt_t = dt.T
    e_cs = jnp.exp(a_cs)
    w_t = jnp.exp(a_cs_t[:, L - 1:L] - a_cs_t) * dt_t
    a_last = a_cs[L - 1:L, :]
    lane = lax.broadcasted_iota(jnp.int32, (L, LANES), 1)
    first_head = lane < SSM_HEAD_DIM
    neg_inf = jnp.float32(-jnp.inf)

    for g in range(G):
        b_g = xc[:, d_inner + g * N:d_inner + (g + 1) * N]
        c_g = xc[:, d_inner + G * N + g * N:d_inner + G * N + (g + 1) * N]
        cb = _dot_nt(c_g.astype(BF16), b_g.astype(BF16))
        b_g_t = b_g.T
        for j in range(pairs_per_group):
            h0 = (g * pairs_per_group + j) * 2
            cols = slice((g * pairs_per_group + j) * LANES, (g * pairs_per_group + j + 1) * LANES)
            x_pair = xc[:, cols].astype(BF16)
            s_pair = state[g, :, j * LANES:(j + 1) * LANES]
            rhs = jnp.concatenate([x_pair, s_pair.astype(BF16)], axis=0)
            ys, sts = [], []
            for hh in (h0, h0 + 1):
                diff = a_cs[:, hh:hh + 1] - a_cs_t[hh:hh + 1, :]
                l_h = cb * jnp.exp(jnp.where(tri, diff, neg_inf)) * dt_t[hh:hh + 1, :]
                c_h = c_g * e_cs[:, hh:hh + 1]
                lhs = jnp.concatenate([l_h.astype(BF16), c_h.astype(BF16)], axis=1)
                ys.append(_dot(lhs, rhs))
                sts.append(_dot((b_g_t * w_t[hh:hh + 1, :]).astype(BF16), x_pair))
            ybuf[:, cols] = jnp.where(first_head, ys[0], ys[1])
            d_a = jnp.where(first_head[0:1, :], jnp.exp(a_last[:, h0:h0 + 1]), jnp.exp(a_last[:, h0 + 1:h0 + 2]))
            state[g, :, j * LANES:(j + 1) * LANES] = s_pair * d_a + jnp.where(first_head, sts[0], sts[1])

    acc = h_ref[...]
    for g in range(G):
        sl = slice(g * gw, (g + 1) * gw)
        zz = z_ref[:, sl]
        gated = (ybuf[:, sl] + xc[:, sl] * dexp_ref[:, sl]) * (zz * _sigmoid(zz))
        gn = gated * lax.rsqrt(jnp.mean(gated * gated, axis=-1, keepdims=True) + SSM_NORM_EPS) * nw_ref[:, sl]
        acc = acc + _dot(gn.astype(BF16), wo_ref[sl, :])
    o_ref[...] = acc


def _ssd_layer(h, z, xbc, dt, conv_w, conv_b, dt_bias, a_log, d_skip, norm_w, out_proj):
    bsz, seq, d = h.shape
    d_inner = z.shape[-1]
    conv_dim = xbc.shape[-1]
    n_conv = conv_w.shape[0]
    L = SSM_CHUNK
    pad = LANES - dt_bias.shape[0]
    row = lambda b, c: (b, c, 0)
    return pl.pallas_call(
        functools.partial(_ssd_kernel, d_inner=d_inner, n_conv=n_conv),
        grid=(bsz, seq // L),
        in_specs=[
            pl.BlockSpec((None, L, d_inner), row),
            pl.BlockSpec((None, L, conv_dim), row),
            pl.BlockSpec((None, L, LANES), row),
            pl.BlockSpec((None, L, d), row),
            _const_spec((n_conv, conv_dim)), _const_spec((1, conv_dim)),
            _const_spec((1, LANES)), _const_spec((1, LANES)),
            _const_spec((1, d_inner)), _const_spec((1, d_inner)),
            _const_spec((d_inner, d)),
        ],
        out_specs=pl.BlockSpec((None, L, d), row),
        out_shape=jax.ShapeDtypeStruct((bsz, seq, d), F32),
        scratch_shapes=[
            pltpu.VMEM((L + CONV_PAD, conv_dim), F32),
            pltpu.VMEM((L, conv_dim), F32),
            pltpu.VMEM((SSM_GROUPS, SSM_STATE, d_inner // SSM_GROUPS), F32),
            pltpu.VMEM((L, d_inner), F32),
        ],
        compiler_params=_params("parallel", "arbitrary"),
        name="ssd_layer",
    )(z, xbc, dt, h, conv_w, conv_b.reshape(1, conv_dim),
      jnp.pad(dt_bias, (0, pad)).reshape(1, LANES), jnp.pad(a_log, (0, pad)).reshape(1, LANES),
      jnp.repeat(d_skip, SSM_HEAD_DIM).reshape(1, d_inner), norm_w.reshape(1, d_inner),
      out_proj.astype(BF16))


def _sb_attn_kernel(q_ref, k_ref, vt_ref, o_ref, run_ref, acc_ref):
    T, DH = SB_TILE, SB_HEAD_DIM
    n_heads = q_ref.shape[1] // DH
    qi = pl.program_id(2)
    lane = lax.broadcasted_iota(jnp.int32, (T, LANES), 1)
    q_heads = []
    for p in range(n_heads // 2):
        q2 = q_ref[:, p * LANES:(p + 1) * LANES]
        q_heads += [jnp.where(lane < DH, q2, jnp.zeros_like(q2)), jnp.where(lane < DH, jnp.zeros_like(q2), q2)]
    row = lax.broadcasted_iota(jnp.int32, (T, T), 0)
    col = lax.broadcasted_iota(jnp.int32, (T, T), 1)
    tri_b = jnp.where(col >= row, 1.0, 0.0).astype(BF16)
    causal = row < col
    heads = range(n_heads)

    def block(tiles):
        starts = [pl.multiple_of(kb * T, T) for kb, _ in tiles]
        streams = [(i, h) for i in range(len(tiles)) for h in heads]
        zs = {(i, h): _dot_nt(k_ref[pl.ds(starts[i], T), (h // 2) * LANES:(h // 2 + 1) * LANES], q_heads[h])
              for i, h in streams}
        his, los = {}, {}
        for i, h in streams:
            z = zs[i, h]
            sp = jnp.maximum(jnp.log2(1.0 + jnp.exp2(jnp.minimum(z, 40.0))), z)
            if tiles[i][1]:
                sp = jnp.where(causal, sp, 0.0)
            his[i, h] = sp.astype(BF16)
            los[i, h] = (sp - his[i, h].astype(F32)).astype(BF16)
        rests = {s: _dot(tri_b, his[s]) + _dot(tri_b, los[s]) for s in streams}
        ws = {}
        for i, h in streams:
            w = jnp.exp2(zs[i, h] - rests[i, h] - run_ref[h:h + 1, :])
            if tiles[i][1]:
                w = jnp.where(causal, w, 0.0)
            ws[i, h] = w.astype(BF16)
            run_ref[h:h + 1, :] += rests[i, h][0:1, :]
        for i, h in streams:
            acc_ref[h * DH:(h + 1) * DH, :] += _dot(vt_ref[h * DH:(h + 1) * DH, pl.ds(starts[i], T)], ws[i, h])

    run_ref[...] = jnp.zeros(run_ref.shape, F32)
    acc_ref[...] = jnp.zeros(acc_ref.shape, F32)

    @pl.when(qi == 0)
    def _():
        block([(qi, True)])

    @pl.when(qi > 0)
    def _():
        block([(qi, True), (qi - 1, False)])

    def all_underflowed():
        return jnp.min(run_ref[0:n_heads, :]) >= SB_UNDERFLOW_LOG2

    def cond(c):
        t, done = c
        return jnp.logical_and(t < qi - 1, jnp.logical_not(done))

    def body(c):
        t, _ = c
        block([(qi - 2 - t, False)])
        return t + 1, all_underflowed()

    lax.while_loop(cond, body, (jnp.int32(0), all_underflowed()))
    o_ref[...] = acc_ref[...].T.astype(o_ref.dtype)


def _sb_attention(q, k, v_t):
    bsz, seq, width = q.shape
    T = SB_TILE
    W = SB_STEP_HEADS * SB_HEAD_DIM
    return pl.pallas_call(
        _sb_attn_kernel,
        grid=(bsz, width // W, seq // T),
        in_specs=[
            pl.BlockSpec((None, T, W), lambda b, p, i: (b, i, p)),
            pl.BlockSpec((None, seq, W), lambda b, p, i: (b, 0, p)),
            pl.BlockSpec((None, W, seq), lambda b, p, i: (b, p, 0)),
        ],
        out_specs=pl.BlockSpec((None, T, W), lambda b, p, i: (b, i, p)),
        out_shape=jax.ShapeDtypeStruct((bsz, seq, width), BF16),
        scratch_shapes=[
            pltpu.VMEM((8, T), F32),
            pltpu.VMEM((W, T), F32),
        ],
        compiler_params=_params("parallel", "parallel", "arbitrary"),
        name="sb_attention",
    )(q, k, v_t)


def _ffn_kernel(*refs, has_mix, final, n_conv):
    refs = list(refs)
    h_ref = refs.pop(0)
    if has_mix:
        mix_ref, wmix_ref = refs.pop(0), refs.pop(0)
    (p_ref, fg_ref, wg_ref, wv_ref, cwg_ref, cwv_ref, cbg_ref, cbv_ref, wd_ref,
     pg_ref, wpg_ref, wpp_ref) = refs[:12]
    refs = refs[12:]
    if final:
        fin_ref = refs.pop(0)
    o_ref, buf_g, buf_v, tail_g, tail_v, acc_ref, hn_ref = refs
    tm = h_ref.shape[0]
    n_chunks = wg_ref.shape[0]
    hist = n_conv - 1
    i = pl.program_id(1)

    @pl.when(i == 0)
    def _():
        tail_g[...] = jnp.zeros(tail_g.shape, F32)
        tail_v[...] = jnp.zeros(tail_v.shape, F32)

    h = h_ref[...]
    if has_mix:
        h = h + _dot(mix_ref[...], wmix_ref[...])
    acc_ref[...] = h
    hn_ref[...] = _rms_norm(h, fg_ref[...], NORM_EPS).astype(BF16)

    def conv(u, buf, tail, cw, cb, ci):
        buf[0:SUBLANES, :] = tail[ci]
        buf[SUBLANES:SUBLANES + tm, :] = u
        tail[ci] = u[tm - SUBLANES:, :]
        w = cw[ci]
        out = cb[ci] + w[hist:hist + 1, :] * u
        for d in range(1, hist + 1):
            out = out + w[hist - d:hist - d + 1, :] * buf[SUBLANES - d:SUBLANES - d + tm, :]
        return out

    def up(ci):
        hn = hn_ref[...]
        return _dot(hn, wg_ref[ci]), _dot(hn, wv_ref[ci])

    ahead = [up(ci) for ci in range(min(FF_LOOKAHEAD, n_chunks))]
    for ci in range(n_chunks):
        u_g, u_v = ahead.pop(0)
        if ci + FF_LOOKAHEAD < n_chunks:
            ahead.append(up(ci + FF_LOOKAHEAD))
        gate = conv(u_g, buf_g.at[ci % 2], tail_g, cwg_ref, cbg_ref, ci)
        val = conv(u_v, buf_v.at[ci % 2], tail_v, cwv_ref, cbv_ref, ci)
        act = (gate * _sigmoid(gate)) * val
        acc_ref[...] += _dot(act.astype(BF16), wd_ref[ci])

    h2 = acc_ref[...]
    gate = _sigmoid(_dot(_rms_norm(h2, pg_ref[...], NORM_EPS).astype(BF16), wpg_ref[...]))
    h3 = h2 + gate * _dot(p_ref[...].astype(BF16), wpp_ref[...])
    if final:
        h3 = _rms_norm(h3, fin_ref[...], NORM_EPS)
    o_ref[...] = h3


def _ffn_layer(h, p, layer, ffn_norm, w_up, conv_w, conv_b, w_down, ple_norm, ple_gate, ple_proj,
               mix=None, w_mix=None, final_norm=None):
    bsz, seq, d = h.shape
    d_ff = w_down.shape[0]
    n_conv = conv_w.shape[0]
    ple_dim = p.shape[-1]
    tm = min(ROW_TILE, seq)
    fc = FF_CHUNK
    n_chunks = d_ff // fc
    assert n_chunks * fc == d_ff

    def chunked_cols(w):
        return w.reshape(w.shape[0], n_chunks, fc).transpose(1, 0, 2)

    row = lambda b, i: (b, i, 0)
    args = [h]
    in_specs = [pl.BlockSpec((None, tm, d), row)]
    if mix is not None:
        args += [mix, w_mix.astype(BF16)]
        in_specs += [pl.BlockSpec((None, tm, mix.shape[-1]), row), _const_spec(w_mix.shape)]
    args += [
        p, ffn_norm.reshape(1, d),
        chunked_cols(w_up[:, :d_ff]).astype(BF16), chunked_cols(w_up[:, d_ff:]).astype(BF16),
        chunked_cols(conv_w[:, :d_ff]), chunked_cols(conv_w[:, d_ff:]),
        chunked_cols(conv_b[None, :d_ff]), chunked_cols(conv_b[None, d_ff:]),
        w_down.reshape(n_chunks, fc, d).astype(BF16),
        ple_norm.reshape(1, d), ple_gate.astype(BF16), ple_proj.astype(BF16),
    ]
    in_specs += [
        pl.BlockSpec((None, None, tm, ple_dim), lambda b, i: (layer, b, i, 0)),
        _const_spec((1, d)),
        _const_spec((n_chunks, d, fc)), _const_spec((n_chunks, d, fc)),
        _const_spec((n_chunks, n_conv, fc)), _const_spec((n_chunks, n_conv, fc)),
        _const_spec((n_chunks, 1, fc)), _const_spec((n_chunks, 1, fc)),
        _const_spec((n_chunks, fc, d)),
        _const_spec((1, d)), _const_spec((d, d)), _const_spec((ple_dim, d)),
    ]
    if final_norm is not None:
        args.append(final_norm.reshape(1, d))
        in_specs.append(_const_spec((1, d)))
    return pl.pallas_call(
        functools.partial(_ffn_kernel, has_mix=mix is not None, final=final_norm is not None, n_conv=n_conv),
        grid=(bsz, seq // tm),
        in_specs=in_specs,
        out_specs=pl.BlockSpec((None, tm, d), row),
        out_shape=jax.ShapeDtypeStruct((bsz, seq, d), F32),
        scratch_shapes=[
            pltpu.VMEM((2, tm + SUBLANES, fc), F32), pltpu.VMEM((2, tm + SUBLANES, fc), F32),
            pltpu.VMEM((n_chunks, SUBLANES, fc), F32),
            pltpu.VMEM((n_chunks, SUBLANES, fc), F32),
            pltpu.VMEM((tm, d), F32),
            pltpu.VMEM((tm, d), BF16),
        ],
        compiler_params=_params("parallel", "arbitrary"),
        name="ffn_layer",
    )(*args)


def kernel(x, p, attn_norm, ffn_norm, ple_norm, ssm_in_proj, ssm_conv_w, ssm_conv_b, ssm_dt_bias, ssm_a_log, ssm_d, ssm_norm, ssm_out_proj, kv_norm, w_kv, w_q, w_o, ffn_up, ffn_conv_w, ffn_conv_b, ffn_down, ple_gate, ple_proj, final_norm):
    depth = attn_norm.shape[0]
    n_a = ssm_in_proj.shape[0]
    d_inner = ssm_out_proj.shape[1]
    conv_dim = ssm_conv_w.shape[-1]
    n_heads = ssm_dt_bias.shape[-1]
    sb_width = w_q.shape[-1]
    q_scale = (SB_HEAD_DIM ** -0.5) * LOG2E

    h = x
    k = v_t = None
    for i in range(depth):
        ffn_args = (ffn_norm[i], ffn_up[i], ffn_conv_w[i], ffn_conv_b[i], ffn_down[i],
                    ple_norm[i], ple_gate[i], ple_proj[i])
        fin = final_norm if i == depth - 1 else None
        if i < n_a:
            w_in = ssm_in_proj[i]
            w_z = w_in[:, :d_inner].astype(BF16)
            w_xbc = w_in[:, d_inner:d_inner + conv_dim].astype(BF16)
            w_dt = jnp.pad(w_in[:, d_inner + conv_dim:], ((0, 0), (0, LANES - n_heads))).astype(BF16)
            z, xbc, dt = _norm_proj(h, attn_norm[i], [w_z, w_xbc, w_dt], [False] * 3, [F32] * 3)
            h = _ssd_layer(h, z, xbc, dt, ssm_conv_w[i], ssm_conv_b[i], ssm_dt_bias[i], ssm_a_log[i],
                           ssm_d[i], ssm_norm[i], ssm_out_proj[i])
            h = _ffn_layer(h, p, i, *ffn_args, final_norm=fin)
        else:
            j = i - n_a
            (q,) = _norm_proj(h, attn_norm[i], [(w_q[j] * q_scale).astype(BF16)], [False], [BF16])
            o = _sb_attention(q, k, v_t)
            h = _ffn_layer(h, p, i, *ffn_args, mix=o, w_mix=w_o[j], final_norm=fin)
        if i == n_a - 1:
            w_k = w_kv[:, :sb_width].astype(BF16)
            w_v_t = w_kv[:, sb_width:].T.astype(BF16)
            k, v_t = _norm_proj(h, kv_norm, [w_k, w_v_t], [False, True], [BF16, BF16])
    return h
```

```python
import functools
import math

import jax
import jax.numpy as jnp
from jax import lax
from jax.experimental import pallas as pl
from jax.experimental.pallas import tpu as pltpu

F32 = jnp.float32
BF16 = jnp.bfloat16

NORM_EPS = 1e-6
SSM_NORM_EPS = 1e-5
SSM_HEAD_DIM = 64
SSM_GROUPS = 4
SSM_STATE = 128
SSM_CHUNK = 128
SB_HEAD_DIM = 64
LANES = 128
SUBLANES = 8
SB_TILE = 256
SB_STEP_HEADS = 4
SB_UNDERFLOW_LOG2 = 160.0
ROW_TILE = 512
FF_CHUNK = 256
FF_LOOKAHEAD = 2
FF_DOWN_GROUP = 3
SSM_SLAB = 512
VMEM_LIMIT = 56 * 1024 * 1024
LOG2E = 1.4426950408889634


def _params(*sem):
    return pltpu.CompilerParams(dimension_semantics=sem, vmem_limit_bytes=VMEM_LIMIT)


def _const_spec(shape):
    nd = len(shape)
    return pl.BlockSpec(shape, lambda *_: (0,) * nd, pipeline_mode=pl.Buffered(1))


def _rms_norm(x, gain, eps):
    return x * lax.rsqrt(jnp.mean(x * x, axis=-1, keepdims=True) + eps) * gain


def _sigmoid(x):
    return 1.0 / (1.0 + jnp.exp(-x))


def _softplus(x):
    return jnp.maximum(x, 0.0) + jnp.log1p(jnp.exp(-jnp.abs(x)))


def _dot(a, b):
    return jnp.dot(a, b, preferred_element_type=F32)


def _dot_nt(a, b):
    return lax.dot_general(a, b, (((1,), (1,)), ((), ())), preferred_element_type=F32)


def _split3(x):
    hi = x.astype(BF16)
    r1 = x - hi.astype(F32)
    mid = r1.astype(BF16)
    lo = (r1 - mid.astype(F32)).astype(BF16)
    return hi, mid, lo


def _norm_proj_kernel(h_ref, g_ref, *refs, transposed, eps):
    n = len(transposed)
    w_refs, o_refs = refs[:n], refs[n:]
    hn = _rms_norm(h_ref[...], g_ref[...], eps).astype(BF16)
    for w_ref, o_ref, tr in zip(w_refs, o_refs, transposed):
        if tr:
            o_ref[...] = _dot_nt(w_ref[...], hn).astype(o_ref.dtype)
        else:
            o_ref[...] = _dot(hn, w_ref[...]).astype(o_ref.dtype)


def _norm_proj(h, gain, weights, transposed, out_dtypes):
    bsz, seq, d = h.shape
    tm = min(ROW_TILE, seq)
    in_specs = [pl.BlockSpec((None, tm, d), lambda b, i: (b, i, 0)), _const_spec((1, d))]
    out_specs, out_shape = [], []
    for w, tr, dt in zip(weights, transposed, out_dtypes):
        in_specs.append(_const_spec(w.shape))
        if tr:
            n = w.shape[0]
            out_specs.append(pl.BlockSpec((None, n, tm), lambda b, i: (b, 0, i)))
            out_shape.append(jax.ShapeDtypeStruct((bsz, n, seq), dt))
        else:
            n = w.shape[1]
            out_specs.append(pl.BlockSpec((None, tm, n), lambda b, i: (b, i, 0)))
            out_shape.append(jax.ShapeDtypeStruct((bsz, seq, n), dt))
    return pl.pallas_call(
        functools.partial(_norm_proj_kernel, transposed=tuple(transposed), eps=NORM_EPS),
        grid=(bsz, seq // tm),
        in_specs=in_specs, out_specs=out_specs, out_shape=out_shape,
        compiler_params=_params("parallel", "parallel"),
        name="norm_proj",
    )(h, gain.reshape(1, d), *weights)


def _ssm_in_kernel(h_ref, g_ref, wz_ref, wx_ref, wdt_ref, cw_ref, cb_ref, z_ref, xc_ref, dt_ref,
                   hn_ref, buf, tail, *, n_conv):
    tm = h_ref.shape[0]
    hist = n_conv - 1
    W = SSM_SLAB
    n_slabs = wx_ref.shape[1] // W
    n_z = wz_ref.shape[1] // W
    i = pl.program_id(1)

    @pl.when(i == 0)
    def _():
        tail[...] = jnp.zeros(tail.shape, F32)

    hn_ref[...] = _rms_norm(h_ref[...], g_ref[...], NORM_EPS).astype(BF16)

    def up(s):
        return _dot(hn_ref[...], wx_ref[:, s * W:(s + 1) * W])

    ahead = [up(s) for s in range(min(FF_LOOKAHEAD, n_slabs))]
    for s in range(n_slabs):
        cols = slice(s * W, (s + 1) * W)
        u = ahead.pop(0)
        if s + FF_LOOKAHEAD < n_slabs:
            ahead.append(up(s + FF_LOOKAHEAD))
        if s < n_z:
            z_ref[:, cols] = _dot(hn_ref[...], wz_ref[:, cols])
        win = buf.at[s % 2]
        win[0:SUBLANES, :] = tail[s]
        win[SUBLANES:SUBLANES + tm, :] = u
        tail[s] = u[tm - SUBLANES:, :]
        acc = cb_ref[:, cols] + cw_ref[hist:hist + 1, cols] * u
        for d in range(1, hist + 1):
            acc = acc + cw_ref[hist - d:hist - d + 1, cols] * win[SUBLANES - d:SUBLANES - d + tm, :]
        xc_ref[:, cols] = acc * _sigmoid(acc)
    for s in range(n_slabs, n_z):
        z_ref[:, s * W:(s + 1) * W] = _dot(hn_ref[...], wz_ref[:, s * W:(s + 1) * W])
    dt_ref[...] = _dot(hn_ref[...], wdt_ref[...])


def _ssm_in_proj(h, gain, w_z, w_xbc, w_dt, conv_w, conv_b):
    bsz, seq, d = h.shape
    tm = min(ROW_TILE, seq)
    d_inner, conv_dim, n_conv = w_z.shape[1], w_xbc.shape[1], conv_w.shape[0]
    assert conv_dim % SSM_SLAB == 0 and d_inner % SSM_SLAB == 0
    row = lambda b, i: (b, i, 0)
    return pl.pallas_call(
        functools.partial(_ssm_in_kernel, n_conv=n_conv),
        grid=(bsz, seq // tm),
        in_specs=[
            pl.BlockSpec((None, tm, d), row), _const_spec((1, d)),
            _const_spec(w_z.shape), _const_spec(w_xbc.shape), _const_spec(w_dt.shape),
            _const_spec((n_conv, conv_dim)), _const_spec((1, conv_dim)),
        ],
        out_specs=[pl.BlockSpec((None, tm, d_inner), row), pl.BlockSpec((None, tm, conv_dim), row),
                   pl.BlockSpec((None, tm, LANES), row)],
        out_shape=[jax.ShapeDtypeStruct((bsz, seq, d_inner), F32), jax.ShapeDtypeStruct((bsz, seq, conv_dim), F32),
                   jax.ShapeDtypeStruct((bsz, seq, LANES), F32)],
        scratch_shapes=[
            pltpu.VMEM((tm, d), BF16),
            pltpu.VMEM((2, tm + SUBLANES, SSM_SLAB), F32),
            pltpu.VMEM((conv_dim // SSM_SLAB, SUBLANES, SSM_SLAB), F32),
        ],
        compiler_params=_params("parallel", "arbitrary"),
        name="ssm_in_proj",
    )(h, gain.reshape(1, d), w_z, w_xbc, w_dt, conv_w, conv_b.reshape(1, conv_dim))


def _ssd_kernel(z_ref, xc, dt_ref, h_ref, dtb_ref, alog_ref, dexp_ref, nw_ref,
                wo_ref, o_ref, state, ybuf, *, d_inner):
    L, N, G = SSM_CHUNK, SSM_STATE, SSM_GROUPS
    gw = d_inner // G
    pairs_per_group = gw // LANES
    c = pl.program_id(1)

    @pl.when(c == 0)
    def _():
        state[...] = jnp.zeros(state.shape, F32)

    dt = _softplus(dt_ref[...] + dtb_ref[...])
    adt = dt * (-jnp.exp(alog_ref[...]))
    row = lax.broadcasted_iota(jnp.int32, (L, L), 0)
    col = lax.broadcasted_iota(jnp.int32, (L, L), 1)
    tri = col <= row
    tri_b = jnp.where(tri, 1.0, 0.0).astype(BF16)
    a_cs = sum(_dot(tri_b, t) for t in _split3(adt))
    a_cs_t = a_cs.T
    dt_t = dt.T
    e_cs = jnp.exp(a_cs)
    w_t = jnp.exp(a_cs_t[:, L - 1:L] - a_cs_t) * dt_t
    a_last = a_cs[L - 1:L, :]
    lane = lax.broadcasted_iota(jnp.int32, (L, LANES), 1)
    first_head = lane < SSM_HEAD_DIM
    neg_inf = jnp.float32(-jnp.inf)

    for g in range(G):
        b_g = xc[:, d_inner + g * N:d_inner + (g + 1) * N]
        c_g = xc[:, d_inner + G * N + g * N:d_inner + G * N + (g + 1) * N]
        cb = _dot_nt(c_g.astype(BF16), b_g.astype(BF16))
        b_g_t = b_g.T
        for j in range(pairs_per_group):
            h0 = (g * pairs_per_group + j) * 2
            cols = slice((g * pairs_per_group + j) * LANES, (g * pairs_per_group + j + 1) * LANES)
            x_pair = xc[:, cols].astype(BF16)
            s_pair = state[g, :, j * LANES:(j + 1) * LANES]
            rhs = jnp.concatenate([x_pair, s_pair.astype(BF16)], axis=0)
            ys, sts = [], []
            for hh in (h0, h0 + 1):
                diff = a_cs[:, hh:hh + 1] - a_cs_t[hh:hh + 1, :]
                l_h = cb * jnp.exp(jnp.where(tri, diff, neg_inf)) * dt_t[hh:hh + 1, :]
                c_h = c_g * e_cs[:, hh:hh + 1]
                lhs = jnp.concatenate([l_h.astype(BF16), c_h.astype(BF16)], axis=1)
                ys.append(_dot(lhs, rhs))
                sts.append(_dot((b_g_t * w_t[hh:hh + 1, :]).astype(BF16), x_pair))
            ybuf[:, cols] = jnp.where(first_head, ys[0], ys[1])
            d_a = jnp.where(first_head[0:1, :], jnp.exp(a_last[:, h0:h0 + 1]), jnp.exp(a_last[:, h0 + 1:h0 + 2]))
            state[g, :, j * LANES:(j + 1) * LANES] = s_pair * d_a + jnp.where(first_head, sts[0], sts[1])

    acc = h_ref[...]
    for g in range(G):
        sl = slice(g * gw, (g + 1) * gw)
        zz = z_ref[:, sl]
        gated = (ybuf[:, sl] + xc[:, sl] * dexp_ref[:, sl]) * (zz * _sigmoid(zz))
        gn = gated * lax.rsqrt(jnp.mean(gated * gated, axis=-1, keepdims=True) + SSM_NORM_EPS) * nw_ref[:, sl]
        acc = acc + _dot(gn.astype(BF16), wo_ref[sl, :])
    o_ref[...] = acc


def _ssd_layer(h, z, xc, dt, dt_bias, a_log, d_skip, norm_w, out_proj):
    bsz, seq, d = h.shape
    d_inner = z.shape[-1]
    conv_dim = xc.shape[-1]
    L = SSM_CHUNK
    pad = LANES - dt_bias.shape[0]
    row = lambda b, c: (b, c, 0)
    return pl.pallas_call(
        functools.partial(_ssd_kernel, d_inner=d_inner),
        grid=(bsz, seq // L),
        in_specs=[
            pl.BlockSpec((None, L, d_inner), row),
            pl.BlockSpec((None, L, conv_dim), row),
            pl.BlockSpec((None, L, LANES), row),
            pl.BlockSpec((None, L, d), row),
            _const_spec((1, LANES)), _const_spec((1, LANES)),
            _const_spec((1, d_inner)), _const_spec((1, d_inner)),
            _const_spec((d_inner, d)),
        ],
        out_specs=pl.BlockSpec((None, L, d), row),
        out_shape=jax.ShapeDtypeStruct((bsz, seq, d), F32),
        scratch_shapes=[
            pltpu.VMEM((SSM_GROUPS, SSM_STATE, d_inner // SSM_GROUPS), F32),
            pltpu.VMEM((L, d_inner), F32),
        ],
        compiler_params=_params("parallel", "arbitrary"),
        name="ssd_layer",
    )(z, xc, dt, h,
      jnp.pad(dt_bias, (0, pad)).reshape(1, LANES), jnp.pad(a_log, (0, pad)).reshape(1, LANES),
      jnp.repeat(d_skip, SSM_HEAD_DIM).reshape(1, d_inner), norm_w.reshape(1, d_inner),
      out_proj.astype(BF16))


def _sb_attn_kernel(q_ref, k_ref, vt_ref, o_ref, run_ref, acc_ref):
    T, DH = SB_TILE, SB_HEAD_DIM
    n_heads = q_ref.shape[1] // DH
    qi = pl.program_id(2)
    lane = lax.broadcasted_iota(jnp.int32, (T, LANES), 1)
    q_heads = []
    for p in range(n_heads // 2):
        q2 = q_ref[:, p * LANES:(p + 1) * LANES]
        q_heads += [jnp.where(lane < DH, q2, jnp.zeros_like(q2)), jnp.where(lane < DH, jnp.zeros_like(q2), q2)]
    row = lax.broadcasted_iota(jnp.int32, (T, T), 0)
    col = lax.broadcasted_iota(jnp.int32, (T, T), 1)
    tri_b = jnp.where(col >= row, 1.0, 0.0).astype(BF16)
    causal = row < col
    heads = range(n_heads)

    def block(tiles):
        starts = [pl.multiple_of(kb * T, T) for kb, _ in tiles]
        streams = [(i, h) for i in range(len(tiles)) for h in heads]
        zs = {(i, h): _dot_nt(k_ref[pl.ds(starts[i], T), (h // 2) * LANES:(h // 2 + 1) * LANES], q_heads[h])
              for i, h in streams}
        his, los = {}, {}
        for i, h in streams:
            z = zs[i, h]
            sp = jnp.maximum(jnp.log2(1.0 + jnp.exp2(jnp.minimum(z, 40.0))), z)
            if tiles[i][1]:
                sp = jnp.where(causal, sp, 0.0)
            his[i, h] = sp.astype(BF16)
            los[i, h] = (sp - his[i, h].astype(F32)).astype(BF16)
        rests = {s: _dot(tri_b, his[s]) + _dot(tri_b, los[s]) for s in streams}
        ws = {}
        for i, h in streams:
            w = jnp.exp2(zs[i, h] - rests[i, h] - run_ref[h:h + 1, :])
            if tiles[i][1]:
                w = jnp.where(causal, w, 0.0)
            ws[i, h] = w.astype(BF16)
            run_ref[h:h + 1, :] += rests[i, h][0:1, :]
        for i, h in streams:
            acc_ref[h * DH:(h + 1) * DH, :] += _dot(vt_ref[h * DH:(h + 1) * DH, pl.ds(starts[i], T)], ws[i, h])

    run_ref[...] = jnp.zeros(run_ref.shape, F32)
    acc_ref[...] = jnp.zeros(acc_ref.shape, F32)

    @pl.when(qi == 0)
    def _():
        block([(qi, True)])

    @pl.when(qi > 0)
    def _():
        block([(qi, True), (qi - 1, False)])

    def all_underflowed():
        return jnp.min(run_ref[0:n_heads, :]) >= SB_UNDERFLOW_LOG2

    def cond(c):
        t, done = c
        return jnp.logical_and(t < qi - 1, jnp.logical_not(done))

    def body(c):
        t, _ = c
        block([(qi - 2 - t, False)])
        return t + 1, all_underflowed()

    lax.while_loop(cond, body, (jnp.int32(0), all_underflowed()))
    o_ref[...] = acc_ref[...].T.astype(o_ref.dtype)


def _sb_attention(q, k, v_t):
    bsz, seq, width = q.shape
    T = SB_TILE
    W = SB_STEP_HEADS * SB_HEAD_DIM
    return pl.pallas_call(
        _sb_attn_kernel,
        grid=(bsz, width // W, seq // T),
        in_specs=[
            pl.BlockSpec((None, T, W), lambda b, p, i: (b, i, p)),
            pl.BlockSpec((None, seq, W), lambda b, p, i: (b, 0, p)),
            pl.BlockSpec((None, W, seq), lambda b, p, i: (b, p, 0)),
        ],
        out_specs=pl.BlockSpec((None, T, W), lambda b, p, i: (b, i, p)),
        out_shape=jax.ShapeDtypeStruct((bsz, seq, width), BF16),
        scratch_shapes=[
            pltpu.VMEM((8, T), F32),
            pltpu.VMEM((W, T), F32),
        ],
        compiler_params=_params("parallel", "parallel", "arbitrary"),
        name="sb_attention",
    )(q, k, v_t)


def _ffn_kernel(*refs, has_mix, final, n_conv):
    refs = list(refs)
    h_ref = refs.pop(0)
    if has_mix:
        mix_ref, wmix_ref = refs.pop(0), refs.pop(0)
    (p_ref, fg_ref, wg_ref, wv_ref, cwg_ref, cwv_ref, cbg_ref, cbv_ref, wd_ref,
     pg_ref, wpg_ref, wpp_ref) = refs[:12]
    refs = refs[12:]
    if final:
        fin_ref = refs.pop(0)
    o_ref, buf_g, buf_v, tail_g, tail_v, acc_ref, hn_ref, act_ref = refs
    tm = h_ref.shape[0]
    n_chunks = wg_ref.shape[0]
    hist = n_conv - 1
    i = pl.program_id(1)

    @pl.when(i == 0)
    def _():
        tail_g[...] = jnp.zeros(tail_g.shape, F32)
        tail_v[...] = jnp.zeros(tail_v.shape, F32)

    h = h_ref[...]
    if has_mix:
        h = h + _dot(mix_ref[...], wmix_ref[...])
    acc_ref[...] = h
    hn_ref[...] = _rms_norm(h, fg_ref[...], NORM_EPS).astype(BF16)

    def conv(u, buf, tail, cw, cb, ci):
        buf[0:SUBLANES, :] = tail[ci]
        buf[SUBLANES:SUBLANES + tm, :] = u
        tail[ci] = u[tm - SUBLANES:, :]
        w = cw[ci]
        out = cb[ci] + w[hist:hist + 1, :] * u
        for d in range(1, hist + 1):
            out = out + w[hist - d:hist - d + 1, :] * buf[SUBLANES - d:SUBLANES - d + tm, :]
        return out

    def up(ci):
        hn = hn_ref[...]
        return _dot(hn, wg_ref[ci]), _dot(hn, wv_ref[ci])

    ahead = [up(ci) for ci in range(min(FF_LOOKAHEAD, n_chunks))]
    for ci in range(n_chunks):
        u_g, u_v = ahead.pop(0)
        if ci + FF_LOOKAHEAD < n_chunks:
            ahead.append(up(ci + FF_LOOKAHEAD))
        gate = conv(u_g, buf_g.at[ci % 2], tail_g, cwg_ref, cbg_ref, ci)
        val = conv(u_v, buf_v.at[ci % 2], tail_v, cwv_ref, cbv_ref, ci)
        act = (gate * _sigmoid(gate)) * val
        act_ref[:, ci * FF_CHUNK:(ci + 1) * FF_CHUNK] = act.astype(BF16)
        if (ci + 1) % FF_DOWN_GROUP == 0 or ci + 1 == n_chunks:
            k0 = (ci // FF_DOWN_GROUP) * FF_DOWN_GROUP * FF_CHUNK
            k1 = (ci + 1) * FF_CHUNK
            acc_ref[...] += _dot(act_ref[:, k0:k1], wd_ref[k0:k1, :])

    h2 = acc_ref[...]
    gate = _sigmoid(_dot(_rms_norm(h2, pg_ref[...], NORM_EPS).astype(BF16), wpg_ref[...]))
    h3 = h2 + gate * _dot(p_ref[...].astype(BF16), wpp_ref[...])
    if final:
        h3 = _rms_norm(h3, fin_ref[...], NORM_EPS)
    o_ref[...] = h3


def _ffn_layer(h, p, layer, ffn_norm, w_up, conv_w, conv_b, w_down, ple_norm, ple_gate, ple_proj,
               mix=None, w_mix=None, final_norm=None):
    bsz, seq, d = h.shape
    d_ff = w_down.shape[0]
    n_conv = conv_w.shape[0]
    ple_dim = p.shape[-1]
    tm = min(ROW_TILE, seq)
    fc = FF_CHUNK
    n_chunks = d_ff // fc
    assert n_chunks * fc == d_ff

    def chunked_cols(w):
        return w.reshape(w.shape[0], n_chunks, fc).transpose(1, 0, 2)

    row = lambda b, i: (b, i, 0)
    args = [h]
    in_specs = [pl.BlockSpec((None, tm, d), row)]
    if mix is not None:
        args += [mix, w_mix.astype(BF16)]
        in_specs += [pl.BlockSpec((None, tm, mix.shape[-1]), row), _const_spec(w_mix.shape)]
    args += [
        p, ffn_norm.reshape(1, d),
        chunked_cols(w_up[:, :d_ff]).astype(BF16), chunked_cols(w_up[:, d_ff:]).astype(BF16),
        chunked_cols(conv_w[:, :d_ff]), chunked_cols(conv_w[:, d_ff:]),
        chunked_cols(conv_b[None, :d_ff]), chunked_cols(conv_b[None, d_ff:]),
        w_down.astype(BF16),
        ple_norm.reshape(1, d), ple_gate.astype(BF16), ple_proj.astype(BF16),
    ]
    in_specs += [
        pl.BlockSpec((None, None, tm, ple_dim), lambda b, i: (layer, b, i, 0)),
        _const_spec((1, d)),
        _const_spec((n_chunks, d, fc)), _const_spec((n_chunks, d, fc)),
        _const_spec((n_chunks, n_conv, fc)), _const_spec((n_chunks, n_conv, fc)),
        _const_spec((n_chunks, 1, fc)), _const_spec((n_chunks, 1, fc)),
        _const_spec((d_ff, d)),
        _const_spec((1, d)), _const_spec((d, d)), _const_spec((ple_dim, d)),
    ]
    if final_norm is not None:
        args.append(final_norm.reshape(1, d))
        in_specs.append(_const_spec((1, d)))
    return pl.pallas_call(
        functools.partial(_ffn_kernel, has_mix=mix is not None, final=final_norm is not None, n_conv=n_conv),
        grid=(bsz, seq // tm),
        in_specs=in_specs,
        out_specs=pl.BlockSpec((None, tm, d), row),
        out_shape=jax.ShapeDtypeStruct((bsz, seq, d), F32),
        scratch_shapes=[
            pltpu.VMEM((2, tm + SUBLANES, fc), F32), pltpu.VMEM((2, tm + SUBLANES, fc), F32),
            pltpu.VMEM((n_chunks, SUBLANES, fc), F32),
            pltpu.VMEM((n_chunks, SUBLANES, fc), F32),
            pltpu.VMEM((tm, d), F32),
            pltpu.VMEM((tm, d), BF16),
            pltpu.VMEM((tm, d_ff), BF16),
        ],
        compiler_params=_params("parallel", "arbitrary"),
        name="ffn_layer",
    )(*args)


def kernel(x, p, attn_norm, ffn_norm, ple_norm, ssm_in_proj, ssm_conv_w, ssm_conv_b, ssm_dt_bias, ssm_a_log, ssm_d, ssm_norm, ssm_out_proj, kv_norm, w_kv, w_q, w_o, ffn_up, ffn_conv_w, ffn_conv_b, ffn_down, ple_gate, ple_proj, final_norm):
    depth = attn_norm.shape[0]
    n_a = ssm_in_proj.shape[0]
    d_inner = ssm_out_proj.shape[1]
    conv_dim = ssm_conv_w.shape[-1]
    n_heads = ssm_dt_bias.shape[-1]
    sb_width = w_q.shape[-1]
    q_scale = (SB_HEAD_DIM ** -0.5) * LOG2E

    h = x
    k = v_t = None
    for i in range(depth):
        ffn_args = (ffn_norm[i], ffn_up[i], ffn_conv_w[i], ffn_conv_b[i], ffn_down[i],
                    ple_norm[i], ple_gate[i], ple_proj[i])
        fin = final_norm if i == depth - 1 else None
        if i < n_a:
            w_in = ssm_in_proj[i]
            w_z = w_in[:, :d_inner].astype(BF16)
            w_xbc = w_in[:, d_inner:d_inner + conv_dim].astype(BF16)
            w_dt = jnp.pad(w_in[:, d_inner + conv_dim:], ((0, 0), (0, LANES - n_heads))).astype(BF16)
            z, xc, dt = _ssm_in_proj(h, attn_norm[i], w_z, w_xbc, w_dt, ssm_conv_w[i], ssm_conv_b[i])
            h = _ssd_layer(h, z, xc, dt, ssm_dt_bias[i], ssm_a_log[i], ssm_d[i], ssm_norm[i], ssm_out_proj[i])
            h = _ffn_layer(h, p, i, *ffn_args, final_norm=fin)
        else:
            j = i - n_a
            (q,) = _norm_proj(h, attn_norm[i], [(w_q[j] * q_scale).astype(BF16)], [False], [BF16])
            o = _sb_attention(q, k, v_t)
            h = _ffn_layer(h, p, i, *ffn_args, mix=o, w_mix=w_o[j], final_norm=fin)
        if i == n_a - 1:
            w_k = w_kv[:, :sb_width].astype(BF16)
            w_v_t = w_kv[:, sb_width:].T.astype(BF16)
            k, v_t = _norm_proj(h, kv_norm, [w_k, w_v_t], [False, True], [BF16, BF16])
    return h
```

```python
import functools
import math

import jax
import jax.numpy as jnp
from jax import lax
from jax.experimental import pallas as pl
from jax.experimental.pallas import tpu as pltpu

F32 = jnp.float32
BF16 = jnp.bfloat16

NORM_EPS = 1e-6
SSM_NORM_EPS = 1e-5
SSM_HEAD_DIM = 64
SSM_GROUPS = 4
SSM_STATE = 128
SSM_CHUNK = 128
SSD_BATCH = 2
SB_HEAD_DIM = 64
LANES = 128
SUBLANES = 8
SB_TILE = 256
SB_STEP_HEADS = 4
SB_UNDERFLOW_LOG2 = 160.0
ROW_TILE = 512
FF_CHUNK = 256
FF_LOOKAHEAD = 2
FF_DOWN_GROUP = 3
SSM_SLAB = 512
VMEM_LIMIT = 56 * 1024 * 1024
LOG2E = 1.4426950408889634


def _params(*sem):
    return pltpu.CompilerParams(dimension_semantics=sem, vmem_limit_bytes=VMEM_LIMIT)


def _const_spec(shape):
    nd = len(shape)
    return pl.BlockSpec(shape, lambda *_: (0,) * nd, pipeline_mode=pl.Buffered(1))


def _rms_norm(x, gain, eps):
    return x * lax.rsqrt(jnp.mean(x * x, axis=-1, keepdims=True) + eps) * gain


def _sigmoid(x):
    return 1.0 / (1.0 + jnp.exp(-x))


def _softplus(x):
    return jnp.maximum(x, 0.0) + jnp.log1p(jnp.exp(-jnp.abs(x)))


def _dot(a, b):
    return jnp.dot(a, b, preferred_element_type=F32)


def _dot_nt(a, b):
    return lax.dot_general(a, b, (((1,), (1,)), ((), ())), preferred_element_type=F32)


def _split3(x):
    hi = x.astype(BF16)
    r1 = x - hi.astype(F32)
    mid = r1.astype(BF16)
    lo = (r1 - mid.astype(F32)).astype(BF16)
    return hi, mid, lo


def _norm_proj_kernel(h_ref, g_ref, *refs, transposed, eps):
    n = len(transposed)
    w_refs, o_refs = refs[:n], refs[n:]
    hn = _rms_norm(h_ref[...], g_ref[...], eps).astype(BF16)
    for w_ref, o_ref, tr in zip(w_refs, o_refs, transposed):
        if tr:
            o_ref[...] = _dot_nt(w_ref[...], hn).astype(o_ref.dtype)
        else:
            o_ref[...] = _dot(hn, w_ref[...]).astype(o_ref.dtype)


def _norm_proj(h, gain, weights, transposed, out_dtypes):
    bsz, seq, d = h.shape
    tm = min(ROW_TILE, seq)
    in_specs = [pl.BlockSpec((None, tm, d), lambda b, i: (b, i, 0)), _const_spec((1, d))]
    out_specs, out_shape = [], []
    for w, tr, dt in zip(weights, transposed, out_dtypes):
        in_specs.append(_const_spec(w.shape))
        if tr:
            n = w.shape[0]
            out_specs.append(pl.BlockSpec((None, n, tm), lambda b, i: (b, 0, i)))
            out_shape.append(jax.ShapeDtypeStruct((bsz, n, seq), dt))
        else:
            n = w.shape[1]
            out_specs.append(pl.BlockSpec((None, tm, n), lambda b, i: (b, i, 0)))
            out_shape.append(jax.ShapeDtypeStruct((bsz, seq, n), dt))
    return pl.pallas_call(
        functools.partial(_norm_proj_kernel, transposed=tuple(transposed), eps=NORM_EPS),
        grid=(bsz, seq // tm),
        in_specs=in_specs, out_specs=out_specs, out_shape=out_shape,
        compiler_params=_params("parallel", "parallel"),
        name="norm_proj",
    )(h, gain.reshape(1, d), *weights)


def _ssm_in_kernel(h_ref, g_ref, wz_ref, wx_ref, wdt_ref, cw_ref, cb_ref, z_ref, xc_ref, dt_ref,
                   hn_ref, buf, tail, *, n_conv):
    tm = h_ref.shape[0]
    hist = n_conv - 1
    W = SSM_SLAB
    n_slabs = wx_ref.shape[1] // W
    n_z = wz_ref.shape[1] // W
    i = pl.program_id(1)

    @pl.when(i == 0)
    def _():
        tail[...] = jnp.zeros(tail.shape, F32)

    hn_ref[...] = _rms_norm(h_ref[...], g_ref[...], NORM_EPS).astype(BF16)

    def up(s):
        return _dot(hn_ref[...], wx_ref[:, s * W:(s + 1) * W])

    ahead = [up(s) for s in range(min(FF_LOOKAHEAD, n_slabs))]
    for s in range(n_slabs):
        cols = slice(s * W, (s + 1) * W)
        u = ahead.pop(0)
        if s + FF_LOOKAHEAD < n_slabs:
            ahead.append(up(s + FF_LOOKAHEAD))
        if s < n_z:
            z_ref[:, cols] = _dot(hn_ref[...], wz_ref[:, cols])
        win = buf.at[s % 2]
        win[0:SUBLANES, :] = tail[s]
        win[SUBLANES:SUBLANES + tm, :] = u
        tail[s] = u[tm - SUBLANES:, :]
        acc = cb_ref[:, cols] + cw_ref[hist:hist + 1, cols] * u
        for d in range(1, hist + 1):
            acc = acc + cw_ref[hist - d:hist - d + 1, cols] * win[SUBLANES - d:SUBLANES - d + tm, :]
        xc_ref[:, cols] = acc * _sigmoid(acc)
    for s in range(n_slabs, n_z):
        z_ref[:, s * W:(s + 1) * W] = _dot(hn_ref[...], wz_ref[:, s * W:(s + 1) * W])
    dt_ref[...] = _dot(hn_ref[...], wdt_ref[...])


def _ssm_in_proj(h, gain, w_z, w_xbc, w_dt, conv_w, conv_b):
    bsz, seq, d = h.shape
    tm = min(ROW_TILE, seq)
    d_inner, conv_dim, n_conv = w_z.shape[1], w_xbc.shape[1], conv_w.shape[0]
    assert conv_dim % SSM_SLAB == 0 and d_inner % SSM_SLAB == 0
    row = lambda b, i: (b, i, 0)
    return pl.pallas_call(
        functools.partial(_ssm_in_kernel, n_conv=n_conv),
        grid=(bsz, seq // tm),
        in_specs=[
            pl.BlockSpec((None, tm, d), row), _const_spec((1, d)),
            _const_spec(w_z.shape), _const_spec(w_xbc.shape), _const_spec(w_dt.shape),
            _const_spec((n_conv, conv_dim)), _const_spec((1, conv_dim)),
        ],
        out_specs=[pl.BlockSpec((None, tm, d_inner), row), pl.BlockSpec((None, tm, conv_dim), row),
                   pl.BlockSpec((None, tm, LANES), row)],
        out_shape=[jax.ShapeDtypeStruct((bsz, seq, d_inner), F32), jax.ShapeDtypeStruct((bsz, seq, conv_dim), F32),
                   jax.ShapeDtypeStruct((bsz, seq, LANES), F32)],
        scratch_shapes=[
            pltpu.VMEM((tm, d), BF16),
            pltpu.VMEM((2, tm + SUBLANES, SSM_SLAB), F32),
            pltpu.VMEM((conv_dim // SSM_SLAB, SUBLANES, SSM_SLAB), F32),
        ],
        compiler_params=_params("parallel", "arbitrary"),
        name="ssm_in_proj",
    )(h, gain.reshape(1, d), w_z, w_xbc, w_dt, conv_w, conv_b.reshape(1, conv_dim))


def _ssd_kernel(z_ref, xc, dt_ref, h_ref, dtb_ref, alog_ref, dexp_ref, nw_ref,
                wo_ref, o_ref, state, ybuf, *, d_inner):
    L, N, G = SSM_CHUNK, SSM_STATE, SSM_GROUPS
    gw = d_inner // G
    pairs_per_group = gw // LANES
    batches = range(z_ref.shape[0])
    c = pl.program_id(1)

    @pl.when(c == 0)
    def _():
        state[...] = jnp.zeros(state.shape, F32)

    row = lax.broadcasted_iota(jnp.int32, (L, L), 0)
    col = lax.broadcasted_iota(jnp.int32, (L, L), 1)
    tri = col <= row
    tri_b = jnp.where(tri, 1.0, 0.0).astype(BF16)
    lane = lax.broadcasted_iota(jnp.int32, (L, LANES), 1)
    first_head = lane < SSM_HEAD_DIM
    neg_inf = jnp.float32(-jnp.inf)
    neg_a = -jnp.exp(alog_ref[...])

    a_cs, a_cs_t, dt_t, e_cs, w_t, a_last = [], [], [], [], [], []
    for b in batches:
        dt = _softplus(dt_ref[b] + dtb_ref[...])
        cs = sum(_dot(tri_b, t) for t in _split3(dt * neg_a))
        a_cs.append(cs)
        a_cs_t.append(cs.T)
        dt_t.append(dt.T)
        e_cs.append(jnp.exp(cs))
        w_t.append(jnp.exp(a_cs_t[b][:, L - 1:L] - a_cs_t[b]) * dt_t[b])
        a_last.append(cs[L - 1:L, :])

    for g in range(G):
        c_g, cb, b_g_t = [], [], []
        for b in batches:
            b_g = xc[b, :, d_inner + g * N:d_inner + (g + 1) * N]
            c_g.append(xc[b, :, d_inner + G * N + g * N:d_inner + G * N + (g + 1) * N])
            cb.append(_dot_nt(c_g[b].astype(BF16), b_g.astype(BF16)))
            b_g_t.append(b_g.T)
        for j in range(pairs_per_group):
            h0 = (g * pairs_per_group + j) * 2
            cols = slice((g * pairs_per_group + j) * LANES, (g * pairs_per_group + j + 1) * LANES)
            for b in batches:
                x_pair = xc[b, :, cols].astype(BF16)
                s_pair = state[b, g, :, j * LANES:(j + 1) * LANES]
                rhs = jnp.concatenate([x_pair, s_pair.astype(BF16)], axis=0)
                ys, sts = [], []
                for hh in (h0, h0 + 1):
                    diff = a_cs[b][:, hh:hh + 1] - a_cs_t[b][hh:hh + 1, :]
                    l_h = cb[b] * jnp.exp(jnp.where(tri, diff, neg_inf)) * dt_t[b][hh:hh + 1, :]
                    c_h = c_g[b] * e_cs[b][:, hh:hh + 1]
                    lhs = jnp.concatenate([l_h.astype(BF16), c_h.astype(BF16)], axis=1)
                    ys.append(_dot(lhs, rhs))
                    sts.append(_dot((b_g_t[b] * w_t[b][hh:hh + 1, :]).astype(BF16), x_pair))
                ybuf[b, :, cols] = jnp.where(first_head, ys[0], ys[1])
                d_a = jnp.where(first_head[0:1, :], jnp.exp(a_last[b][:, h0:h0 + 1]),
                                jnp.exp(a_last[b][:, h0 + 1:h0 + 2]))
                state[b, g, :, j * LANES:(j + 1) * LANES] = s_pair * d_a + jnp.where(first_head, sts[0], sts[1])

    acc = None
    for g in range(G):
        sl = slice(g * gw, (g + 1) * gw)
        gns = []
        for b in batches:
            zz = z_ref[b, :, sl]
            gated = (ybuf[b, :, sl] + xc[b, :, sl] * dexp_ref[:, sl]) * (zz * _sigmoid(zz))
            gn = gated * lax.rsqrt(jnp.mean(gated * gated, axis=-1, keepdims=True) + SSM_NORM_EPS) * nw_ref[:, sl]
            gns.append(gn.astype(BF16))
        part = _dot(jnp.concatenate(gns, axis=0), wo_ref[sl, :])
        acc = part if acc is None else acc + part
    for b in batches:
        o_ref[b] = h_ref[b] + acc[b * L:(b + 1) * L, :]


def _ssd_layer(h, z, xc, dt, dt_bias, a_log, d_skip, norm_w, out_proj):
    bsz, seq, d = h.shape
    d_inner = z.shape[-1]
    conv_dim = xc.shape[-1]
    L = SSM_CHUNK
    nb = SSD_BATCH if bsz % SSD_BATCH == 0 else 1
    pad = LANES - dt_bias.shape[0]
    row = lambda b, c: (b, c, 0)
    return pl.pallas_call(
        functools.partial(_ssd_kernel, d_inner=d_inner),
        grid=(bsz // nb, seq // L),
        in_specs=[
            pl.BlockSpec((nb, L, d_inner), row),
            pl.BlockSpec((nb, L, conv_dim), row),
            pl.BlockSpec((nb, L, LANES), row),
            pl.BlockSpec((nb, L, d), row),
            _const_spec((1, LANES)), _const_spec((1, LANES)),
            _const_spec((1, d_inner)), _const_spec((1, d_inner)),
            _const_spec((d_inner, d)),
        ],
        out_specs=pl.BlockSpec((nb, L, d), row),
        out_shape=jax.ShapeDtypeStruct((bsz, seq, d), F32),
        scratch_shapes=[
            pltpu.VMEM((nb, SSM_GROUPS, SSM_STATE, d_inner // SSM_GROUPS), F32),
            pltpu.VMEM((nb, L, d_inner), F32),
        ],
        compiler_params=_params("parallel", "arbitrary"),
        name="ssd_layer",
    )(z, xc, dt, h,
      jnp.pad(dt_bias, (0, pad)).reshape(1, LANES), jnp.pad(a_log, (0, pad)).reshape(1, LANES),
      jnp.repeat(d_skip, SSM_HEAD_DIM).reshape(1, d_inner), norm_w.reshape(1, d_inner),
      out_proj.astype(BF16))


def _sb_attn_kernel(q_ref, k_ref, vt_ref, o_ref, run_ref, acc_ref):
    T, DH = SB_TILE, SB_HEAD_DIM
    n_heads = q_ref.shape[1] // DH
    qi = pl.program_id(2)
    lane = lax.broadcasted_iota(jnp.int32, (T, LANES), 1)
    q_heads = []
    for p in range(n_heads // 2):
        q2 = q_ref[:, p * LANES:(p + 1) * LANES]
        q_heads += [jnp.where(lane < DH, q2, jnp.zeros_like(q2)), jnp.where(lane < DH, jnp.zeros_like(q2), q2)]
    row = lax.broadcasted_iota(jnp.int32, (T, T), 0)
    col = lax.broadcasted_iota(jnp.int32, (T, T), 1)
    tri_b = jnp.where(col >= row, 1.0, 0.0).astype(BF16)
    causal = row < col
    heads = range(n_heads)

    def block(tiles):
        starts = [pl.multiple_of(kb * T, T) for kb, _ in tiles]
        streams = [(i, h) for i in range(len(tiles)) for h in heads]
        zs = {(i, h): _dot_nt(k_ref[pl.ds(starts[i], T), (h // 2) * LANES:(h // 2 + 1) * LANES], q_heads[h])
              for i, h in streams}
        sps = {}
        for i, h in streams:
            z = zs[i, h]
            sp = jnp.maximum(jnp.log2(1.0 + jnp.exp2(jnp.minimum(z, 40.0))), z)
            if tiles[i][1]:
                sp = jnp.where(causal, sp, 0.0)
            sps[i, h] = sp.astype(BF16)
        rests = {s: _dot(tri_b, sps[s]) for s in streams}
        ws = {}
        for i, h in streams:
            w = jnp.exp2(zs[i, h] - rests[i, h] - run_ref[h:h + 1, :])
            if tiles[i][1]:
                w = jnp.where(causal, w, 0.0)
            ws[i, h] = w.astype(BF16)
            run_ref[h:h + 1, :] += rests[i, h][0:1, :]
        for i, h in streams:
            acc_ref[h * DH:(h + 1) * DH, :] += _dot(vt_ref[h * DH:(h + 1) * DH, pl.ds(starts[i], T)], ws[i, h])

    run_ref[...] = jnp.zeros(run_ref.shape, F32)
    acc_ref[...] = jnp.zeros(acc_ref.shape, F32)

    @pl.when(qi == 0)
    def _():
        block([(qi, True)])

    @pl.when(qi > 0)
    def _():
        block([(qi, True), (qi - 1, False)])

    def all_underflowed():
        return jnp.min(run_ref[0:n_heads, :]) >= SB_UNDERFLOW_LOG2

    def cond(c):
        t, done = c
        return jnp.logical_and(t < qi - 1, jnp.logical_not(done))

    def body(c):
        t, _ = c
        block([(qi - 2 - t, False)])
        return t + 1, all_underflowed()

    lax.while_loop(cond, body, (jnp.int32(0), all_underflowed()))
    o_ref[...] = acc_ref[...].T.astype(o_ref.dtype)


def _sb_attention(q, k, v_t):
    bsz, seq, width = q.shape
    T = SB_TILE
    W = SB_STEP_HEADS * SB_HEAD_DIM
    return pl.pallas_call(
        _sb_attn_kernel,
        grid=(bsz, width // W, seq // T),
        in_specs=[
            pl.BlockSpec((None, T, W), lambda b, p, i: (b, i, p)),
            pl.BlockSpec((None, seq, W), lambda b, p, i: (b, 0, p)),
            pl.BlockSpec((None, W, seq), lambda b, p, i: (b, p, 0)),
        ],
        out_specs=pl.BlockSpec((None, T, W), lambda b, p, i: (b, i, p)),
        out_shape=jax.ShapeDtypeStruct((bsz, seq, width), BF16),
        scratch_shapes=[
            pltpu.VMEM((8, T), F32),
            pltpu.VMEM((W, T), F32),
        ],
        compiler_params=_params("parallel", "parallel", "arbitrary"),
        name="sb_attention",
    )(q, k, v_t)


def _ffn_kernel(*refs, has_mix, final, n_conv):
    refs = list(refs)
    h_ref = refs.pop(0)
    if has_mix:
        mix_ref, wmix_ref = refs.pop(0), refs.pop(0)
    (p_ref, fg_ref, wg_ref, wv_ref, cwg_ref, cwv_ref, cbg_ref, cbv_ref, wd_ref,
     pg_ref, wpg_ref, wpp_ref) = refs[:12]
    refs = refs[12:]
    if final:
        fin_ref = refs.pop(0)
    o_ref, buf_g, buf_v, tail_g, tail_v, acc_ref, hn_ref, act_ref = refs
    tm = h_ref.shape[0]
    n_chunks = wg_ref.shape[0]
    hist = n_conv - 1
    i = pl.program_id(1)

    @pl.when(i == 0)
    def _():
        tail_g[...] = jnp.zeros(tail_g.shape, F32)
        tail_v[...] = jnp.zeros(tail_v.shape, F32)

    h = h_ref[...]
    if has_mix:
        h = h + _dot(mix_ref[...], wmix_ref[...])
    acc_ref[...] = h
    hn_ref[...] = _rms_norm(h, fg_ref[...], NORM_EPS).astype(BF16)

    def conv(u, buf, tail, cw, cb, ci):
        buf[0:SUBLANES, :] = tail[ci]
        buf[SUBLANES:SUBLANES + tm, :] = u
        tail[ci] = u[tm - SUBLANES:, :]
        w = cw[ci]
        out = cb[ci] + w[hist:hist + 1, :] * u
        for d in range(1, hist + 1):
            out = out + w[hist - d:hist - d + 1, :] * buf[SUBLANES - d:SUBLANES - d + tm, :]
        return out

    def up(ci):
        hn = hn_ref[...]
        return _dot(hn, wg_ref[ci]), _dot(hn, wv_ref[ci])

    ahead = [up(ci) for ci in range(min(FF_LOOKAHEAD, n_chunks))]
    for ci in range(n_chunks):
        u_g, u_v = ahead.pop(0)
        if ci + FF_LOOKAHEAD < n_chunks:
            ahead.append(up(ci + FF_LOOKAHEAD))
        gate = conv(u_g, buf_g.at[ci % 2], tail_g, cwg_ref, cbg_ref, ci)
        val = conv(u_v, buf_v.at[ci % 2], tail_v, cwv_ref, cbv_ref, ci)
        act = (gate * _sigmoid(gate)) * val
        act_ref[:, ci * FF_CHUNK:(ci + 1) * FF_CHUNK] = act.astype(BF16)
        if (ci + 1) % FF_DOWN_GROUP == 0 or ci + 1 == n_chunks:
            k0 = (ci // FF_DOWN_GROUP) * FF_DOWN_GROUP * FF_CHUNK
            k1 = (ci + 1) * FF_CHUNK
            acc_ref[...] += _dot(act_ref[:, k0:k1], wd_ref[k0:k1, :])

    h2 = acc_ref[...]
    gate = _sigmoid(_dot(_rms_norm(h2, pg_ref[...], NORM_EPS).astype(BF16), wpg_ref[...]))
    h3 = h2 + gate * _dot(p_ref[...].astype(BF16), wpp_ref[...])
    if final:
        h3 = _rms_norm(h3, fin_ref[...], NORM_EPS)
    o_ref[...] = h3


def _ffn_layer(h, p, layer, ffn_norm, w_up, conv_w, conv_b, w_down, ple_norm, ple_gate, ple_proj,
               mix=None, w_mix=None, final_norm=None):
    bsz, seq, d = h.shape
    d_ff = w_down.shape[0]
    n_conv = conv_w.shape[0]
    ple_dim = p.shape[-1]
    tm = min(ROW_TILE, seq)
    fc = FF_CHUNK
    n_chunks = d_ff // fc
    assert n_chunks * fc == d_ff

    def chunked_cols(w):
        return w.reshape(w.shape[0], n_chunks, fc).transpose(1, 0, 2)

    row = lambda b, i: (b, i, 0)
    args = [h]
    in_specs = [pl.BlockSpec((None, tm, d), row)]
    if mix is not None:
        args += [mix, w_mix.astype(BF16)]
        in_specs += [pl.BlockSpec((None, tm, mix.shape[-1]), row), _const_spec(w_mix.shape)]
    args += [
        p, ffn_norm.reshape(1, d),
        chunked_cols(w_up[:, :d_ff]).astype(BF16), chunked_cols(w_up[:, d_ff:]).astype(BF16),
        chunked_cols(conv_w[:, :d_ff]), chunked_cols(conv_w[:, d_ff:]),
        chunked_cols(conv_b[None, :d_ff]), chunked_cols(conv_b[None, d_ff:]),
        w_down.astype(BF16),
        ple_norm.reshape(1, d), ple_gate.astype(BF16), ple_proj.astype(BF16),
    ]
    in_specs += [
        pl.BlockSpec((None, None, tm, ple_dim), lambda b, i: (layer, b, i, 0)),
        _const_spec((1, d)),
        _const_spec((n_chunks, d, fc)), _const_spec((n_chunks, d, fc)),
        _const_spec((n_chunks, n_conv, fc)), _const_spec((n_chunks, n_conv, fc)),
        _const_spec((n_chunks, 1, fc)), _const_spec((n_chunks, 1, fc)),
        _const_spec((d_ff, d)),
        _const_spec((1, d)), _const_spec((d, d)), _const_spec((ple_dim, d)),
    ]
    if final_norm is not None:
        args.append(final_norm.reshape(1, d))
        in_specs.append(_const_spec((1, d)))
    return pl.pallas_call(
        functools.partial(_ffn_kernel, has_mix=mix is not None, final=final_norm is not None, n_conv=n_conv),
        grid=(bsz, seq // tm),
        in_specs=in_specs,
        out_specs=pl.BlockSpec((None, tm, d), row),
        out_shape=jax.ShapeDtypeStruct((bsz, seq, d), F32),
        scratch_shapes=[
            pltpu.VMEM((2, tm + SUBLANES, fc), F32), pltpu.VMEM((2, tm + SUBLANES, fc), F32),
            pltpu.VMEM((n_chunks, SUBLANES, fc), F32),
            pltpu.VMEM((n_chunks, SUBLANES, fc), F32),
            pltpu.VMEM((tm, d), F32),
            pltpu.VMEM((tm, d), BF16),
            pltpu.VMEM((tm, d_ff), BF16),
        ],
        compiler_params=_params("parallel", "arbitrary"),
        name="ffn_layer",
    )(*args)


def kernel(x, p, attn_norm, ffn_norm, ple_norm, ssm_in_proj, ssm_conv_w, ssm_conv_b, ssm_dt_bias, ssm_a_log, ssm_d, ssm_norm, ssm_out_proj, kv_norm, w_kv, w_q, w_o, ffn_up, ffn_conv_w, ffn_conv_b, ffn_down, ple_gate, ple_proj, final_norm):
    depth = attn_norm.shape[0]
    n_a = ssm_in_proj.shape[0]
    d_inner = ssm_out_proj.shape[1]
    conv_dim = ssm_conv_w.shape[-1]
    n_heads = ssm_dt_bias.shape[-1]
    sb_width = w_q.shape[-1]
    q_scale = (SB_HEAD_DIM ** -0.5) * LOG2E

    h = x
    k = v_t = None
    for i in range(depth):
        ffn_args = (ffn_norm[i], ffn_up[i], ffn_conv_w[i], ffn_conv_b[i], ffn_down[i],
                    ple_norm[i], ple_gate[i], ple_proj[i])
        fin = final_norm if i == depth - 1 else None
        if i < n_a:
            w_in = ssm_in_proj[i]
            w_z = w_in[:, :d_inner].astype(BF16)
            w_xbc = w_in[:, d_inner:d_inner + conv_dim].astype(BF16)
            w_dt = jnp.pad(w_in[:, d_inner + conv_dim:], ((0, 0), (0, LANES - n_heads))).astype(BF16)
            z, xc, dt = _ssm_in_proj(h, attn_norm[i], w_z, w_xbc, w_dt, ssm_conv_w[i], ssm_conv_b[i])
            h = _ssd_layer(h, z, xc, dt, ssm_dt_bias[i], ssm_a_log[i], ssm_d[i], ssm_norm[i], ssm_out_proj[i])
            h = _ffn_layer(h, p, i, *ffn_args, final_norm=fin)
        else:
            j = i - n_a
            (q,) = _norm_proj(h, attn_norm[i], [(w_q[j] * q_scale).astype(BF16)], [False], [BF16])
            o = _sb_attention(q, k, v_t)
            h = _ffn_layer(h, p, i, *ffn_args, mix=o, w_mix=w_o[j], final_norm=fin)
        if i == n_a - 1:
            w_k = w_kv[:, :sb_width].astype(BF16)
            w_v_t = w_kv[:, sb_width:].T.astype(BF16)
            k, v_t = _norm_proj(h, kv_norm, [w_k, w_v_t], [False, True], [BF16, BF16])
    return h
```

```python
import functools
import math

import jax
import jax.numpy as jnp
from jax import lax
from jax.experimental import pallas as pl
from jax.experimental.pallas import tpu as pltpu

F32 = jnp.float32
BF16 = jnp.bfloat16

NORM_EPS = 1e-6
SSM_NORM_EPS = 1e-5
SSM_HEAD_DIM = 64
SSM_GROUPS = 4
SSM_STATE = 128
SSM_CHUNK = 128
SSD_BATCH = 2
SB_HEAD_DIM = 64
LANES = 128
SUBLANES = 8
SB_TILE = 256
SB_STEP_HEADS = 8
SB_UNDERFLOW_LOG2 = 160.0
ROW_TILE = 512
FF_CHUNK = 256
FF_LOOKAHEAD = 2
FF_DOWN_GROUP = 3
SSM_SLAB = 256
VMEM_LIMIT = 56 * 1024 * 1024
LOG2E = 1.4426950408889634


def _params(*sem):
    return pltpu.CompilerParams(dimension_semantics=sem, vmem_limit_bytes=VMEM_LIMIT)


def _const_spec(shape):
    nd = len(shape)
    return pl.BlockSpec(shape, lambda *_: (0,) * nd, pipeline_mode=pl.Buffered(1))


def _rms_norm(x, gain, eps):
    return x * lax.rsqrt(jnp.mean(x * x, axis=-1, keepdims=True) + eps) * gain


def _sigmoid(x):
    return 1.0 / (1.0 + jnp.exp(-x))


def _softplus(x):
    return jnp.maximum(x, 0.0) + jnp.log1p(jnp.exp(-jnp.abs(x)))


def _dot(a, b):
    return jnp.dot(a, b, preferred_element_type=F32)


def _dot_nt(a, b):
    return lax.dot_general(a, b, (((1,), (1,)), ((), ())), preferred_element_type=F32)


def _split3(x):
    hi = x.astype(BF16)
    r1 = x - hi.astype(F32)
    mid = r1.astype(BF16)
    lo = (r1 - mid.astype(F32)).astype(BF16)
    return hi, mid, lo


def _norm_proj_kernel(h_ref, g_ref, *refs, transposed, eps):
    n = len(transposed)
    w_refs, o_refs = refs[:n], refs[n:]
    hn = _rms_norm(h_ref[...], g_ref[...], eps).astype(BF16)
    for w_ref, o_ref, tr in zip(w_refs, o_refs, transposed):
        if tr:
            o_ref[...] = _dot_nt(w_ref[...], hn).astype(o_ref.dtype)
        else:
            o_ref[...] = _dot(hn, w_ref[...]).astype(o_ref.dtype)


def _norm_proj(h, gain, weights, transposed, out_dtypes):
    bsz, seq, d = h.shape
    tm = min(ROW_TILE, seq)
    in_specs = [pl.BlockSpec((None, tm, d), lambda b, i: (b, i, 0)), _const_spec((1, d))]
    out_specs, out_shape = [], []
    for w, tr, dt in zip(weights, transposed, out_dtypes):
        in_specs.append(_const_spec(w.shape))
        if tr:
            n = w.shape[0]
            out_specs.append(pl.BlockSpec((None, n, tm), lambda b, i: (b, 0, i)))
            out_shape.append(jax.ShapeDtypeStruct((bsz, n, seq), dt))
        else:
            n = w.shape[1]
            out_specs.append(pl.BlockSpec((None, tm, n), lambda b, i: (b, i, 0)))
            out_shape.append(jax.ShapeDtypeStruct((bsz, seq, n), dt))
    return pl.pallas_call(
        functools.partial(_norm_proj_kernel, transposed=tuple(transposed), eps=NORM_EPS),
        grid=(bsz, seq // tm),
        in_specs=in_specs, out_specs=out_specs, out_shape=out_shape,
        compiler_params=_params("parallel", "parallel"),
        name="norm_proj",
    )(h, gain.reshape(1, d), *weights)


def _ssm_in_kernel(h_ref, g_ref, wz_ref, wx_ref, wdt_ref, cw_ref, cb_ref, z_ref, xc_ref, dt_ref,
                   hn_ref, buf, tail, *, n_conv):
    tm = h_ref.shape[0]
    hist = n_conv - 1
    W = SSM_SLAB
    n_slabs = wx_ref.shape[1] // W
    n_z = wz_ref.shape[1] // W
    i = pl.program_id(1)

    @pl.when(i == 0)
    def _():
        tail[...] = jnp.zeros(tail.shape, F32)

    hn_ref[...] = _rms_norm(h_ref[...], g_ref[...], NORM_EPS).astype(BF16)

    def up(s):
        return _dot(hn_ref[...], wx_ref[:, s * W:(s + 1) * W])

    ahead = [up(s) for s in range(min(FF_LOOKAHEAD, n_slabs))]
    for s in range(n_slabs):
        cols = slice(s * W, (s + 1) * W)
        u = ahead.pop(0)
        if s + FF_LOOKAHEAD < n_slabs:
            ahead.append(up(s + FF_LOOKAHEAD))
        if s < n_z:
            z_ref[:, cols] = _dot(hn_ref[...], wz_ref[:, cols])
        win = buf.at[s % 2]
        win[0:SUBLANES, :] = tail[s]
        win[SUBLANES:SUBLANES + tm, :] = u
        tail[s] = u[tm - SUBLANES:, :]
        acc = cb_ref[:, cols] + cw_ref[hist:hist + 1, cols] * u
        for d in range(1, hist + 1):
            acc = acc + cw_ref[hist - d:hist - d + 1, cols] * win[SUBLANES - d:SUBLANES - d + tm, :]
        xc_ref[:, cols] = acc * _sigmoid(acc)
    for s in range(n_slabs, n_z):
        z_ref[:, s * W:(s + 1) * W] = _dot(hn_ref[...], wz_ref[:, s * W:(s + 1) * W])
    dt_ref[...] = _dot(hn_ref[...], wdt_ref[...])


def _ssm_in_proj(h, gain, w_z, w_xbc, w_dt, conv_w, conv_b):
    bsz, seq, d = h.shape
    tm = min(ROW_TILE, seq)
    d_inner, conv_dim, n_conv = w_z.shape[1], w_xbc.shape[1], conv_w.shape[0]
    assert conv_dim % SSM_SLAB == 0 and d_inner % SSM_SLAB == 0
    row = lambda b, i: (b, i, 0)
    return pl.pallas_call(
        functools.partial(_ssm_in_kernel, n_conv=n_conv),
        grid=(bsz, seq // tm),
        in_specs=[
            pl.BlockSpec((None, tm, d), row), _const_spec((1, d)),
            _const_spec(w_z.shape), _const_spec(w_xbc.shape), _const_spec(w_dt.shape),
            _const_spec((n_conv, conv_dim)), _const_spec((1, conv_dim)),
        ],
        out_specs=[pl.BlockSpec((None, tm, d_inner), row), pl.BlockSpec((None, tm, conv_dim), row),
                   pl.BlockSpec((None, tm, LANES), row)],
        out_shape=[jax.ShapeDtypeStruct((bsz, seq, d_inner), F32), jax.ShapeDtypeStruct((bsz, seq, conv_dim), F32),
                   jax.ShapeDtypeStruct((bsz, seq, LANES), F32)],
        scratch_shapes=[
            pltpu.VMEM((tm, d), BF16),
            pltpu.VMEM((2, tm + SUBLANES, SSM_SLAB), F32),
            pltpu.VMEM((conv_dim // SSM_SLAB, SUBLANES, SSM_SLAB), F32),
        ],
        compiler_params=_params("parallel", "arbitrary"),
        name="ssm_in_proj",
    )(h, gain.reshape(1, d), w_z, w_xbc, w_dt, conv_w, conv_b.reshape(1, conv_dim))


def _ssd_kernel(z_ref, xc, dt_ref, h_ref, dtb_ref, alog_ref, dexp_ref, nw_ref,
                wo_ref, o_ref, state, ybuf, *, d_inner):
    L, N, G = SSM_CHUNK, SSM_STATE, SSM_GROUPS
    gw = d_inner // G
    pairs_per_group = gw // LANES
    batches = range(z_ref.shape[0])
    c = pl.program_id(1)

    @pl.when(c == 0)
    def _():
        state[...] = jnp.zeros(state.shape, F32)

    row = lax.broadcasted_iota(jnp.int32, (L, L), 0)
    col = lax.broadcasted_iota(jnp.int32, (L, L), 1)
    tri = col <= row
    tri_b = jnp.where(tri, 1.0, 0.0).astype(BF16)
    lane = lax.broadcasted_iota(jnp.int32, (L, LANES), 1)
    first_head = lane < SSM_HEAD_DIM
    neg_inf = jnp.float32(-jnp.inf)
    neg_a = -jnp.exp(alog_ref[...])

    a_cs, a_cs_t, dt_t, e_cs, w_t, a_last = [], [], [], [], [], []
    for b in batches:
        dt = _softplus(dt_ref[b] + dtb_ref[...])
        cs = sum(_dot(tri_b, t) for t in _split3(dt * neg_a))
        a_cs.append(cs)
        a_cs_t.append(cs.T)
        dt_t.append(dt.T)
        e_cs.append(jnp.exp(cs))
        w_t.append(jnp.exp(a_cs_t[b][:, L - 1:L] - a_cs_t[b]) * dt_t[b])
        a_last.append(cs[L - 1:L, :])

    for g in range(G):
        c_g, cb, b_g_t = [], [], []
        for b in batches:
            b_g = xc[b, :, d_inner + g * N:d_inner + (g + 1) * N]
            c_g.append(xc[b, :, d_inner + G * N + g * N:d_inner + G * N + (g + 1) * N])
            cb.append(_dot_nt(c_g[b].astype(BF16), b_g.astype(BF16)))
            b_g_t.append(b_g.T)
        for j in range(pairs_per_group):
            h0 = (g * pairs_per_group + j) * 2
            cols = slice((g * pairs_per_group + j) * LANES, (g * pairs_per_group + j + 1) * LANES)
            for b in batches:
                x_pair = xc[b, :, cols].astype(BF16)
                s_pair = state[b, g, :, j * LANES:(j + 1) * LANES]
                rhs = jnp.concatenate([x_pair, s_pair.astype(BF16)], axis=0)
                ys, sts = [], []
                for hh in (h0, h0 + 1):
                    diff = a_cs[b][:, hh:hh + 1] - a_cs_t[b][hh:hh + 1, :]
                    l_h = cb[b] * jnp.exp(jnp.where(tri, diff, neg_inf)) * dt_t[b][hh:hh + 1, :]
                    c_h = c_g[b] * e_cs[b][:, hh:hh + 1]
                    lhs = jnp.concatenate([l_h.astype(BF16), c_h.astype(BF16)], axis=1)
                    ys.append(_dot(lhs, rhs))
                    sts.append(_dot((b_g_t[b] * w_t[b][hh:hh + 1, :]).astype(BF16), x_pair))
                ybuf[b, :, cols] = jnp.where(first_head, ys[0], ys[1])
                d_a = jnp.where(first_head[0:1, :], jnp.exp(a_last[b][:, h0:h0 + 1]),
                                jnp.exp(a_last[b][:, h0 + 1:h0 + 2]))
                state[b, g, :, j * LANES:(j + 1) * LANES] = s_pair * d_a + jnp.where(first_head, sts[0], sts[1])

    acc = None
    for g in range(G):
        sl = slice(g * gw, (g + 1) * gw)
        gns = []
        for b in batches:
            zz = z_ref[b, :, sl]
            gated = (ybuf[b, :, sl] + xc[b, :, sl] * dexp_ref[:, sl]) * (zz * _sigmoid(zz))
            gn = gated * lax.rsqrt(jnp.mean(gated * gated, axis=-1, keepdims=True) + SSM_NORM_EPS) * nw_ref[:, sl]
            gns.append(gn.astype(BF16))
        part = _dot(jnp.concatenate(gns, axis=0), wo_ref[sl, :])
        acc = part if acc is None else acc + part
    for b in batches:
        o_ref[b] = h_ref[b] + acc[b * L:(b + 1) * L, :]


def _ssd_layer(h, z, xc, dt, dt_bias, a_log, d_skip, norm_w, out_proj):
    bsz, seq, d = h.shape
    d_inner = z.shape[-1]
    conv_dim = xc.shape[-1]
    L = SSM_CHUNK
    nb = SSD_BATCH if bsz % SSD_BATCH == 0 else 1
    pad = LANES - dt_bias.shape[0]
    row = lambda b, c: (b, c, 0)
    return pl.pallas_call(
        functools.partial(_ssd_kernel, d_inner=d_inner),
        grid=(bsz // nb, seq // L),
        in_specs=[
            pl.BlockSpec((nb, L, d_inner), row),
            pl.BlockSpec((nb, L, conv_dim), row),
            pl.BlockSpec((nb, L, LANES), row),
            pl.BlockSpec((nb, L, d), row),
            _const_spec((1, LANES)), _const_spec((1, LANES)),
            _const_spec((1, d_inner)), _const_spec((1, d_inner)),
            _const_spec((d_inner, d)),
        ],
        out_specs=pl.BlockSpec((nb, L, d), row),
        out_shape=jax.ShapeDtypeStruct((bsz, seq, d), F32),
        scratch_shapes=[
            pltpu.VMEM((nb, SSM_GROUPS, SSM_STATE, d_inner // SSM_GROUPS), F32),
            pltpu.VMEM((nb, L, d_inner), F32),
        ],
        compiler_params=_params("parallel", "arbitrary"),
        name="ssd_layer",
    )(z, xc, dt, h,
      jnp.pad(dt_bias, (0, pad)).reshape(1, LANES), jnp.pad(a_log, (0, pad)).reshape(1, LANES),
      jnp.repeat(d_skip, SSM_HEAD_DIM).reshape(1, d_inner), norm_w.reshape(1, d_inner),
      out_proj.astype(BF16))


def _sb_attn_kernel(q_ref, k_ref, vt_ref, o_ref, run_ref, acc_ref):
    T, DH = SB_TILE, SB_HEAD_DIM
    n_heads = q_ref.shape[1] // DH
    qi = pl.program_id(2)
    lane = lax.broadcasted_iota(jnp.int32, (T, LANES), 1)
    q_heads = []
    for p in range(n_heads // 2):
        q2 = q_ref[:, p * LANES:(p + 1) * LANES]
        q_heads += [jnp.where(lane < DH, q2, jnp.zeros_like(q2)), jnp.where(lane < DH, jnp.zeros_like(q2), q2)]
    row = lax.broadcasted_iota(jnp.int32, (T, T), 0)
    col = lax.broadcasted_iota(jnp.int32, (T, T), 1)
    tri_b = jnp.where(col >= row, 1.0, 0.0).astype(BF16)
    causal = row < col
    heads = range(n_heads)

    def block(tiles):
        starts = [pl.multiple_of(kb * T, T) for kb, _ in tiles]
        streams = [(i, h) for i in range(len(tiles)) for h in heads]
        zs = {(i, h): _dot_nt(k_ref[pl.ds(starts[i], T), (h // 2) * LANES:(h // 2 + 1) * LANES], q_heads[h])
              for i, h in streams}
        sps = {}
        for i, h in streams:
            z = zs[i, h]
            sp = jnp.maximum(jnp.log2(1.0 + jnp.exp2(jnp.minimum(z, 40.0))), z)
            if tiles[i][1]:
                sp = jnp.where(causal, sp, 0.0)
            sps[i, h] = sp.astype(BF16)
        rests = {s: _dot(tri_b, sps[s]) for s in streams}
        ws = {}
        for i, h in streams:
            w = jnp.exp2(zs[i, h] - rests[i, h] - run_ref[h:h + 1, :])
            if tiles[i][1]:
                w = jnp.where(causal, w, 0.0)
            ws[i, h] = w.astype(BF16)
            run_ref[h:h + 1, :] += rests[i, h][0:1, :]
        for i, h in streams:
            acc_ref[h * DH:(h + 1) * DH, :] += _dot(vt_ref[h * DH:(h + 1) * DH, pl.ds(starts[i], T)], ws[i, h])

    run_ref[...] = jnp.zeros(run_ref.shape, F32)
    acc_ref[...] = jnp.zeros(acc_ref.shape, F32)

    @pl.when(qi == 0)
    def _():
        block([(qi, True)])

    @pl.when(qi > 0)
    def _():
        block([(qi, True), (qi - 1, False)])

    def all_underflowed():
        return jnp.min(run_ref[0:n_heads, :]) >= SB_UNDERFLOW_LOG2

    def cond(c):
        t, done = c
        return jnp.logical_and(t < qi - 1, jnp.logical_not(done))

    def body(c):
        t, _ = c
        block([(qi - 2 - t, False)])
        return t + 1, all_underflowed()

    lax.while_loop(cond, body, (jnp.int32(0), all_underflowed()))
    o_ref[...] = acc_ref[...].T.astype(o_ref.dtype)


def _sb_attention(q, k, v_t):
    bsz, seq, width = q.shape
    T = SB_TILE
    W = SB_STEP_HEADS * SB_HEAD_DIM
    return pl.pallas_call(
        _sb_attn_kernel,
        grid=(bsz, width // W, seq // T),
        in_specs=[
            pl.BlockSpec((None, T, W), lambda b, p, i: (b, i, p)),
            pl.BlockSpec((None, seq, W), lambda b, p, i: (b, 0, p)),
            pl.BlockSpec((None, W, seq), lambda b, p, i: (b, p, 0)),
        ],
        out_specs=pl.BlockSpec((None, T, W), lambda b, p, i: (b, i, p)),
        out_shape=jax.ShapeDtypeStruct((bsz, seq, width), BF16),
        scratch_shapes=[
            pltpu.VMEM((max(SB_STEP_HEADS, SUBLANES), T), F32),
            pltpu.VMEM((W, T), F32),
        ],
        compiler_params=_params("parallel", "parallel", "arbitrary"),
        name="sb_attention",
    )(q, k, v_t)


def _ffn_kernel(*refs, has_mix, final, n_conv):
    refs = list(refs)
    h_ref = refs.pop(0)
    if has_mix:
        mix_ref, wmix_ref = refs.pop(0), refs.pop(0)
    (p_ref, fg_ref, wg_ref, wv_ref, cwg_ref, cwv_ref, cbg_ref, cbv_ref, wd_ref,
     pg_ref, wpg_ref, wpp_ref) = refs[:12]
    refs = refs[12:]
    if final:
        fin_ref = refs.pop(0)
    o_ref, buf_g, buf_v, tail_g, tail_v, acc_ref, hn_ref, act_ref = refs
    tm = h_ref.shape[0]
    n_chunks = wg_ref.shape[0]
    hist = n_conv - 1
    i = pl.program_id(1)

    @pl.when(i == 0)
    def _():
        tail_g[...] = jnp.zeros(tail_g.shape, F32)
        tail_v[...] = jnp.zeros(tail_v.shape, F32)

    h = h_ref[...]
    if has_mix:
        h = h + _dot(mix_ref[...], wmix_ref[...])
    acc_ref[...] = h
    hn_ref[...] = _rms_norm(h, fg_ref[...], NORM_EPS).astype(BF16)

    def conv(u, buf, tail, cw, cb, ci):
        buf[0:SUBLANES, :] = tail[ci]
        buf[SUBLANES:SUBLANES + tm, :] = u
        tail[ci] = u[tm - SUBLANES:, :]
        w = cw[ci]
        out = cb[ci] + w[hist:hist + 1, :] * u
        for d in range(1, hist + 1):
            out = out + w[hist - d:hist - d + 1, :] * buf[SUBLANES - d:SUBLANES - d + tm, :]
        return out

    def up(ci):
        hn = hn_ref[...]
        return _dot(hn, wg_ref[ci]), _dot(hn, wv_ref[ci])

    ahead = [up(ci) for ci in range(min(FF_LOOKAHEAD, n_chunks))]
    for ci in range(n_chunks):
        u_g, u_v = ahead.pop(0)
        if ci + FF_LOOKAHEAD < n_chunks:
            ahead.append(up(ci + FF_LOOKAHEAD))
        gate = conv(u_g, buf_g.at[ci % 2], tail_g, cwg_ref, cbg_ref, ci)
        val = conv(u_v, buf_v.at[ci % 2], tail_v, cwv_ref, cbv_ref, ci)
        act = (gate * _sigmoid(gate)) * val
        act_ref[:, ci * FF_CHUNK:(ci + 1) * FF_CHUNK] = act.astype(BF16)
        if (ci + 1) % FF_DOWN_GROUP == 0 or ci + 1 == n_chunks:
            k0 = (ci // FF_DOWN_GROUP) * FF_DOWN_GROUP * FF_CHUNK
            k1 = (ci + 1) * FF_CHUNK
            acc_ref[...] += _dot(act_ref[:, k0:k1], wd_ref[k0:k1, :])

    h2 = acc_ref[...]
    gate = _sigmoid(_dot(_rms_norm(h2, pg_ref[...], NORM_EPS).astype(BF16), wpg_ref[...]))
    h3 = h2 + gate * _dot(p_ref[...].astype(BF16), wpp_ref[...])
    if final:
        h3 = _rms_norm(h3, fin_ref[...], NORM_EPS)
    o_ref[...] = h3


def _ffn_layer(h, p, layer, ffn_norm, w_up, conv_w, conv_b, w_down, ple_norm, ple_gate, ple_proj,
               mix=None, w_mix=None, final_norm=None):
    bsz, seq, d = h.shape
    d_ff = w_down.shape[0]
    n_conv = conv_w.shape[0]
    ple_dim = p.shape[-1]
    tm = min(ROW_TILE, seq)
    fc = FF_CHUNK
    n_chunks = d_ff // fc
    assert n_chunks * fc == d_ff

    def chunked_cols(w):
        return w.reshape(w.shape[0], n_chunks, fc).transpose(1, 0, 2)

    row = lambda b, i: (b, i, 0)
    args = [h]
    in_specs = [pl.BlockSpec((None, tm, d), row)]
    if mix is not None:
        args += [mix, w_mix.astype(BF16)]
        in_specs += [pl.BlockSpec((None, tm, mix.shape[-1]), row), _const_spec(w_mix.shape)]
    args += [
        p, ffn_norm.reshape(1, d),
        chunked_cols(w_up[:, :d_ff]).astype(BF16), chunked_cols(w_up[:, d_ff:]).astype(BF16),
        chunked_cols(conv_w[:, :d_ff]), chunked_cols(conv_w[:, d_ff:]),
        chunked_cols(conv_b[None, :d_ff]), chunked_cols(conv_b[None, d_ff:]),
        w_down.astype(BF16),
        ple_norm.reshape(1, d), ple_gate.astype(BF16), ple_proj.astype(BF16),
    ]
    in_specs += [
        pl.BlockSpec((None, None, tm, ple_dim), lambda b, i: (layer, b, i, 0)),
        _const_spec((1, d)),
        _const_spec((n_chunks, d, fc)), _const_spec((n_chunks, d, fc)),
        _const_spec((n_chunks, n_conv, fc)), _const_spec((n_chunks, n_conv, fc)),
        _const_spec((n_chunks, 1, fc)), _const_spec((n_chunks, 1, fc)),
        _const_spec((d_ff, d)),
        _const_spec((1, d)), _const_spec((d, d)), _const_spec((ple_dim, d)),
    ]
    if final_norm is not None:
        args.append(final_norm.reshape(1, d))
        in_specs.append(_const_spec((1, d)))
    return pl.pallas_call(
        functools.partial(_ffn_kernel, has_mix=mix is not None, final=final_norm is not None, n_conv=n_conv),
        grid=(bsz, seq // tm),
        in_specs=in_specs,
        out_specs=pl.BlockSpec((None, tm, d), row),
        out_shape=jax.ShapeDtypeStruct((bsz, seq, d), F32),
        scratch_shapes=[
            pltpu.VMEM((2, tm + SUBLANES, fc), F32), pltpu.VMEM((2, tm + SUBLANES, fc), F32),
            pltpu.VMEM((n_chunks, SUBLANES, fc), F32),
            pltpu.VMEM((n_chunks, SUBLANES, fc), F32),
            pltpu.VMEM((tm, d), F32),
            pltpu.VMEM((tm, d), BF16),
            pltpu.VMEM((tm, d_ff), BF16),
        ],
        compiler_params=_params("parallel", "arbitrary"),
        name="ffn_layer",
    )(*args)


def kernel(x, p, attn_norm, ffn_norm, ple_norm, ssm_in_proj, ssm_conv_w, ssm_conv_b, ssm_dt_bias, ssm_a_log, ssm_d, ssm_norm, ssm_out_proj, kv_norm, w_kv, w_q, w_o, ffn_up, ffn_conv_w, ffn_conv_b, ffn_down, ple_gate, ple_proj, final_norm):
    depth = attn_norm.shape[0]
    n_a = ssm_in_proj.shape[0]
    d_inner = ssm_out_proj.shape[1]
    conv_dim = ssm_conv_w.shape[-1]
    n_heads = ssm_dt_bias.shape[-1]
    sb_width = w_q.shape[-1]
    q_scale = (SB_HEAD_DIM ** -0.5) * LOG2E

    h = x
    k = v_t = None
    for i in range(depth):
        ffn_args = (ffn_norm[i], ffn_up[i], ffn_conv_w[i], ffn_conv_b[i], ffn_down[i],
                    ple_norm[i], ple_gate[i], ple_proj[i])
        fin = final_norm if i == depth - 1 else None
        if i < n_a:
            w_in = ssm_in_proj[i]
            w_z = w_in[:, :d_inner].astype(BF16)
            w_xbc = w_in[:, d_inner:d_inner + conv_dim].astype(BF16)
            w_dt = jnp.pad(w_in[:, d_inner + conv_dim:], ((0, 0), (0, LANES - n_heads))).astype(BF16)
            z, xc, dt = _ssm_in_proj(h, attn_norm[i], w_z, w_xbc, w_dt, ssm_conv_w[i], ssm_conv_b[i])
            h = _ssd_layer(h, z, xc, dt, ssm_dt_bias[i], ssm_a_log[i], ssm_d[i], ssm_norm[i], ssm_out_proj[i])
            h = _ffn_layer(h, p, i, *ffn_args, final_norm=fin)
        else:
            j = i - n_a
            (q,) = _norm_proj(h, attn_norm[i], [(w_q[j] * q_scale).astype(BF16)], [False], [BF16])
            o = _sb_attention(q, k, v_t)
            h = _ffn_layer(h, p, i, *ffn_args, mix=o, w_mix=w_o[j], final_norm=fin)
        if i == n_a - 1:
            w_k = w_kv[:, :sb_width].astype(BF16)
            w_v_t = w_kv[:, sb_width:].T.astype(BF16)
            k, v_t = _norm_proj(h, kv_norm, [w_k, w_v_t], [False, True], [BF16, BF16])
    return h
```

```python
import functools
import math

import jax
import jax.numpy as jnp
from jax import lax
from jax.experimental import pallas as pl
from jax.experimental.pallas import tpu as pltpu

F32 = jnp.float32
BF16 = jnp.bfloat16

NORM_EPS = 1e-6
SSM_NORM_EPS = 1e-5
SSM_HEAD_DIM = 64
SSM_GROUPS = 4
SSM_STATE = 128
SSM_CHUNK = 128
SSD_BATCH = 4
SB_HEAD_DIM = 64
LANES = 128
SUBLANES = 8
SB_TILE = 256
SB_STEP_HEADS = 8
SB_UNDERFLOW_LOG2 = 160.0
ROW_TILE = 512
FF_CHUNK = 256
FF_LOOKAHEAD = 2
FF_DOWN_GROUP = 11
SSM_SLAB = 256
VMEM_LIMIT = 56 * 1024 * 1024
LOG2E = 1.4426950408889634


def _params(*sem):
    return pltpu.CompilerParams(dimension_semantics=sem, vmem_limit_bytes=VMEM_LIMIT)


def _const_spec(shape):
    nd = len(shape)
    return pl.BlockSpec(shape, lambda *_: (0,) * nd, pipeline_mode=pl.Buffered(1))


def _rms_norm(x, gain, eps):
    return x * lax.rsqrt(jnp.mean(x * x, axis=-1, keepdims=True) + eps) * gain


def _sigmoid(x):
    return 1.0 / (1.0 + jnp.exp(-x))


def _softplus(x):
    return jnp.maximum(x, 0.0) + jnp.log1p(jnp.exp(-jnp.abs(x)))


def _dot(a, b):
    return jnp.dot(a, b, preferred_element_type=F32)


def _dot_nt(a, b):
    return lax.dot_general(a, b, (((1,), (1,)), ((), ())), preferred_element_type=F32)


def _split3(x):
    hi = x.astype(BF16)
    r1 = x - hi.astype(F32)
    mid = r1.astype(BF16)
    lo = (r1 - mid.astype(F32)).astype(BF16)
    return hi, mid, lo


def _norm_proj_kernel(h_ref, g_ref, *refs, transposed, eps):
    n = len(transposed)
    w_refs, o_refs = refs[:n], refs[n:]
    hn = _rms_norm(h_ref[...], g_ref[...], eps).astype(BF16)
    for w_ref, o_ref, tr in zip(w_refs, o_refs, transposed):
        if tr:
            o_ref[...] = _dot_nt(w_ref[...], hn).astype(o_ref.dtype)
        else:
            o_ref[...] = _dot(hn, w_ref[...]).astype(o_ref.dtype)


def _norm_proj(h, gain, weights, transposed, out_dtypes):
    bsz, seq, d = h.shape
    tm = min(ROW_TILE, seq)
    in_specs = [pl.BlockSpec((None, tm, d), lambda b, i: (b, i, 0)), _const_spec((1, d))]
    out_specs, out_shape = [], []
    for w, tr, dt in zip(weights, transposed, out_dtypes):
        in_specs.append(_const_spec(w.shape))
        if tr:
            n = w.shape[0]
            out_specs.append(pl.BlockSpec((None, n, tm), lambda b, i: (b, 0, i)))
            out_shape.append(jax.ShapeDtypeStruct((bsz, n, seq), dt))
        else:
            n = w.shape[1]
            out_specs.append(pl.BlockSpec((None, tm, n), lambda b, i: (b, i, 0)))
            out_shape.append(jax.ShapeDtypeStruct((bsz, seq, n), dt))
    return pl.pallas_call(
        functools.partial(_norm_proj_kernel, transposed=tuple(transposed), eps=NORM_EPS),
        grid=(bsz, seq // tm),
        in_specs=in_specs, out_specs=out_specs, out_shape=out_shape,
        compiler_params=_params("parallel", "parallel"),
        name="norm_proj",
    )(h, gain.reshape(1, d), *weights)


def _ssm_in_kernel(h_ref, g_ref, wz_ref, wx_ref, wdt_ref, cw_ref, cb_ref, z_ref, xc_ref, dt_ref,
                   hn_ref, buf, tail, *, n_conv):
    tm = h_ref.shape[0]
    hist = n_conv - 1
    W = SSM_SLAB
    n_slabs = wx_ref.shape[1] // W
    n_z = wz_ref.shape[1] // W
    i = pl.program_id(1)

    @pl.when(i == 0)
    def _():
        tail[...] = jnp.zeros(tail.shape, F32)

    hn_ref[...] = _rms_norm(h_ref[...], g_ref[...], NORM_EPS).astype(BF16)

    def up(s):
        return _dot(hn_ref[...], wx_ref[:, s * W:(s + 1) * W])

    ahead = [up(s) for s in range(min(FF_LOOKAHEAD, n_slabs))]
    for s in range(n_slabs):
        cols = slice(s * W, (s + 1) * W)
        u = ahead.pop(0)
        if s + FF_LOOKAHEAD < n_slabs:
            ahead.append(up(s + FF_LOOKAHEAD))
        if s < n_z:
            z_ref[:, cols] = _dot(hn_ref[...], wz_ref[:, cols])
        win = buf.at[s % 2]
        win[0:SUBLANES, :] = tail[s]
        win[SUBLANES:SUBLANES + tm, :] = u
        tail[s] = u[tm - SUBLANES:, :]
        acc = cb_ref[:, cols] + cw_ref[hist:hist + 1, cols] * u
        for d in range(1, hist + 1):
            acc = acc + cw_ref[hist - d:hist - d + 1, cols] * win[SUBLANES - d:SUBLANES - d + tm, :]
        xc_ref[:, cols] = acc * _sigmoid(acc)
    for s in range(n_slabs, n_z):
        z_ref[:, s * W:(s + 1) * W] = _dot(hn_ref[...], wz_ref[:, s * W:(s + 1) * W])
    dt_ref[...] = _dot(hn_ref[...], wdt_ref[...])


def _ssm_in_proj(h, gain, w_z, w_xbc, w_dt, conv_w, conv_b):
    bsz, seq, d = h.shape
    tm = min(ROW_TILE, seq)
    d_inner, conv_dim, n_conv = w_z.shape[1], w_xbc.shape[1], conv_w.shape[0]
    assert conv_dim % SSM_SLAB == 0 and d_inner % SSM_SLAB == 0
    row = lambda b, i: (b, i, 0)
    return pl.pallas_call(
        functools.partial(_ssm_in_kernel, n_conv=n_conv),
        grid=(bsz, seq // tm),
        in_specs=[
            pl.BlockSpec((None, tm, d), row), _const_spec((1, d)),
            _const_spec(w_z.shape), _const_spec(w_xbc.shape), _const_spec(w_dt.shape),
            _const_spec((n_conv, conv_dim)), _const_spec((1, conv_dim)),
        ],
        out_specs=[pl.BlockSpec((None, tm, d_inner), row), pl.BlockSpec((None, tm, conv_dim), row),
                   pl.BlockSpec((None, tm, LANES), row)],
        out_shape=[jax.ShapeDtypeStruct((bsz, seq, d_inner), F32), jax.ShapeDtypeStruct((bsz, seq, conv_dim), F32),
                   jax.ShapeDtypeStruct((bsz, seq, LANES), F32)],
        scratch_shapes=[
            pltpu.VMEM((tm, d), BF16),
            pltpu.VMEM((2, tm + SUBLANES, SSM_SLAB), F32),
            pltpu.VMEM((conv_dim // SSM_SLAB, SUBLANES, SSM_SLAB), F32),
        ],
        compiler_params=_params("parallel", "arbitrary"),
        name="ssm_in_proj",
    )(h, gain.reshape(1, d), w_z, w_xbc, w_dt, conv_w, conv_b.reshape(1, conv_dim))


def _ssd_kernel(z_ref, xc, dt_ref, h_ref, dtb_ref, alog_ref, dexp_ref, nw_ref,
                wo_ref, o_ref, state, ybuf, *, d_inner):
    L, N, G = SSM_CHUNK, SSM_STATE, SSM_GROUPS
    gw = d_inner // G
    pairs_per_group = gw // LANES
    batches = range(z_ref.shape[0])
    c = pl.program_id(1)

    @pl.when(c == 0)
    def _():
        state[...] = jnp.zeros(state.shape, F32)

    row = lax.broadcasted_iota(jnp.int32, (L, L), 0)
    col = lax.broadcasted_iota(jnp.int32, (L, L), 1)
    tri = col <= row
    tri_b = jnp.where(tri, 1.0, 0.0).astype(BF16)
    lane = lax.broadcasted_iota(jnp.int32, (L, LANES), 1)
    first_head = lane < SSM_HEAD_DIM
    neg_inf = jnp.float32(-jnp.inf)
    neg_a = -jnp.exp(alog_ref[...])

    a_cs, a_cs_t, dt_t, e_cs, w_t, a_last = [], [], [], [], [], []
    for b in batches:
        dt = _softplus(dt_ref[b] + dtb_ref[...])
        cs = sum(_dot(tri_b, t) for t in _split3(dt * neg_a))
        a_cs.append(cs)
        a_cs_t.append(cs.T)
        dt_t.append(dt.T)
        e_cs.append(jnp.exp(cs))
        w_t.append(jnp.exp(a_cs_t[b][:, L - 1:L] - a_cs_t[b]) * dt_t[b])
        a_last.append(cs[L - 1:L, :])

    for g in range(G):
        c_g, cb, b_g_t = [], [], []
        for b in batches:
            b_g = xc[b, :, d_inner + g * N:d_inner + (g + 1) * N]
            c_g.append(xc[b, :, d_inner + G * N + g * N:d_inner + G * N + (g + 1) * N])
            cb.append(_dot_nt(c_g[b].astype(BF16), b_g.astype(BF16)))
            b_g_t.append(b_g.T)
        for j in range(pairs_per_group):
            h0 = (g * pairs_per_group + j) * 2
            cols = slice((g * pairs_per_group + j) * LANES, (g * pairs_per_group + j + 1) * LANES)
            for b in batches:
                x_pair = xc[b, :, cols].astype(BF16)
                s_pair = state[b, g, :, j * LANES:(j + 1) * LANES]
                rhs = jnp.concatenate([x_pair, s_pair.astype(BF16)], axis=0)
                ys, sts = [], []
                for hh in (h0, h0 + 1):
                    diff = a_cs[b][:, hh:hh + 1] - a_cs_t[b][hh:hh + 1, :]
                    l_h = cb[b] * jnp.exp(jnp.where(tri, diff, neg_inf)) * dt_t[b][hh:hh + 1, :]
                    c_h = c_g[b] * e_cs[b][:, hh:hh + 1]
                    lhs = jnp.concatenate([l_h.astype(BF16), c_h.astype(BF16)], axis=1)
                    ys.append(_dot(lhs, rhs))
                    sts.append(_dot((b_g_t[b] * w_t[b][hh:hh + 1, :]).astype(BF16), x_pair))
                ybuf[b, :, cols] = jnp.where(first_head, ys[0], ys[1])
                d_a = jnp.where(first_head[0:1, :], jnp.exp(a_last[b][:, h0:h0 + 1]),
                                jnp.exp(a_last[b][:, h0 + 1:h0 + 2]))
                state[b, g, :, j * LANES:(j + 1) * LANES] = s_pair * d_a + jnp.where(first_head, sts[0], sts[1])

    acc = None
    for g in range(G):
        sl = slice(g * gw, (g + 1) * gw)
        gns = []
        for b in batches:
            zz = z_ref[b, :, sl]
            gated = (ybuf[b, :, sl] + xc[b, :, sl] * dexp_ref[:, sl]) * (zz * _sigmoid(zz))
            gn = gated * lax.rsqrt(jnp.mean(gated * gated, axis=-1, keepdims=True) + SSM_NORM_EPS) * nw_ref[:, sl]
            gns.append(gn.astype(BF16))
        part = _dot(jnp.concatenate(gns, axis=0), wo_ref[sl, :])
        acc = part if acc is None else acc + part
    for b in batches:
        o_ref[b] = h_ref[b] + acc[b * L:(b + 1) * L, :]


def _ssd_layer(h, z, xc, dt, dt_bias, a_log, d_skip, norm_w, out_proj):
    bsz, seq, d = h.shape
    d_inner = z.shape[-1]
    conv_dim = xc.shape[-1]
    L = SSM_CHUNK
    nb = SSD_BATCH if bsz % SSD_BATCH == 0 else 1
    pad = LANES - dt_bias.shape[0]
    row = lambda b, c: (b, c, 0)
    return pl.pallas_call(
        functools.partial(_ssd_kernel, d_inner=d_inner),
        grid=(bsz // nb, seq // L),
        in_specs=[
            pl.BlockSpec((nb, L, d_inner), row),
            pl.BlockSpec((nb, L, conv_dim), row),
            pl.BlockSpec((nb, L, LANES), row),
            pl.BlockSpec((nb, L, d), row),
            _const_spec((1, LANES)), _const_spec((1, LANES)),
            _const_spec((1, d_inner)), _const_spec((1, d_inner)),
            _const_spec((d_inner, d)),
        ],
        out_specs=pl.BlockSpec((nb, L, d), row),
        out_shape=jax.ShapeDtypeStruct((bsz, seq, d), F32),
        scratch_shapes=[
            pltpu.VMEM((nb, SSM_GROUPS, SSM_STATE, d_inner // SSM_GROUPS), F32),
            pltpu.VMEM((nb, L, d_inner), F32),
        ],
        compiler_params=_params("parallel", "arbitrary"),
        name="ssd_layer",
    )(z, xc, dt, h,
      jnp.pad(dt_bias, (0, pad)).reshape(1, LANES), jnp.pad(a_log, (0, pad)).reshape(1, LANES),
      jnp.repeat(d_skip, SSM_HEAD_DIM).reshape(1, d_inner), norm_w.reshape(1, d_inner),
      out_proj.astype(BF16))


def _sb_attn_kernel(q_ref, k_ref, vt_ref, o_ref, run_ref, acc_ref):
    T, DH = SB_TILE, SB_HEAD_DIM
    n_heads = q_ref.shape[1] // DH
    qi = pl.program_id(2)
    lane = lax.broadcasted_iota(jnp.int32, (T, LANES), 1)
    q_heads = []
    for p in range(n_heads // 2):
        q2 = q_ref[:, p * LANES:(p + 1) * LANES]
        q_heads += [jnp.where(lane < DH, q2, jnp.zeros_like(q2)), jnp.where(lane < DH, jnp.zeros_like(q2), q2)]
    row = lax.broadcasted_iota(jnp.int32, (T, T), 0)
    col = lax.broadcasted_iota(jnp.int32, (T, T), 1)
    tri_b = jnp.where(col >= row, 1.0, 0.0).astype(BF16)
    causal = row < col
    heads = range(n_heads)

    def block(tiles):
        starts = [pl.multiple_of(kb * T, T) for kb, _ in tiles]
        streams = [(i, h) for i in range(len(tiles)) for h in heads]
        zs = {(i, h): _dot_nt(k_ref[pl.ds(starts[i], T), (h // 2) * LANES:(h // 2 + 1) * LANES], q_heads[h])
              for i, h in streams}
        sps = {}
        for i, h in streams:
            z = zs[i, h]
            sp = jnp.maximum(jnp.log2(1.0 + jnp.exp2(jnp.minimum(z, 40.0))), z)
            if tiles[i][1]:
                sp = jnp.where(causal, sp, 0.0)
            sps[i, h] = sp.astype(BF16)
        rests = {s: _dot(tri_b, sps[s]) for s in streams}
        ws = {}
        for i, h in streams:
            w = jnp.exp2(zs[i, h] - rests[i, h] - run_ref[h:h + 1, :])
            if tiles[i][1]:
                w = jnp.where(causal, w, 0.0)
            ws[i, h] = w.astype(BF16)
            run_ref[h:h + 1, :] += rests[i, h][0:1, :]
        for i, h in streams:
            acc_ref[h * DH:(h + 1) * DH, :] += _dot(vt_ref[h * DH:(h + 1) * DH, pl.ds(starts[i], T)], ws[i, h])

    run_ref[...] = jnp.zeros(run_ref.shape, F32)
    acc_ref[...] = jnp.zeros(acc_ref.shape, F32)

    @pl.when(qi == 0)
    def _():
        block([(qi, True)])

    @pl.when(qi > 0)
    def _():
        block([(qi, True), (qi - 1, False)])

    def all_underflowed():
        return jnp.min(run_ref[0:n_heads, :]) >= SB_UNDERFLOW_LOG2

    def cond(c):
        t, done = c
        return jnp.logical_and(t < qi - 1, jnp.logical_not(done))

    def body(c):
        t, _ = c
        block([(qi - 2 - t, False)])
        return t + 1, all_underflowed()

    lax.while_loop(cond, body, (jnp.int32(0), all_underflowed()))
    o_ref[...] = acc_ref[...].T.astype(o_ref.dtype)


def _sb_attention(q, k, v_t):
    bsz, seq, width = q.shape
    T = SB_TILE
    W = SB_STEP_HEADS * SB_HEAD_DIM
    return pl.pallas_call(
        _sb_attn_kernel,
        grid=(bsz, width // W, seq // T),
        in_specs=[
            pl.BlockSpec((None, T, W), lambda b, p, i: (b, i, p)),
            pl.BlockSpec((None, seq, W), lambda b, p, i: (b, 0, p)),
            pl.BlockSpec((None, W, seq), lambda b, p, i: (b, p, 0)),
        ],
        out_specs=pl.BlockSpec((None, T, W), lambda b, p, i: (b, i, p)),
        out_shape=jax.ShapeDtypeStruct((bsz, seq, width), BF16),
        scratch_shapes=[
            pltpu.VMEM((max(SB_STEP_HEADS, SUBLANES), T), F32),
            pltpu.VMEM((W, T), F32),
        ],
        compiler_params=_params("parallel", "parallel", "arbitrary"),
        name="sb_attention",
    )(q, k, v_t)


def _ffn_kernel(*refs, has_mix, final, n_conv):
    refs = list(refs)
    h_ref = refs.pop(0)
    if has_mix:
        mix_ref, wmix_ref = refs.pop(0), refs.pop(0)
    (p_ref, fg_ref, wg_ref, wv_ref, cwg_ref, cwv_ref, cbg_ref, cbv_ref, wd_ref,
     pg_ref, wpg_ref, wpp_ref) = refs[:12]
    refs = refs[12:]
    if final:
        fin_ref = refs.pop(0)
    o_ref, buf_g, buf_v, tail_g, tail_v, acc_ref, hn_ref, act_ref = refs
    tm = h_ref.shape[0]
    n_chunks = wg_ref.shape[0]
    hist = n_conv - 1
    i = pl.program_id(1)

    @pl.when(i == 0)
    def _():
        tail_g[...] = jnp.zeros(tail_g.shape, F32)
        tail_v[...] = jnp.zeros(tail_v.shape, F32)

    h = h_ref[...]
    if has_mix:
        h = h + _dot(mix_ref[...], wmix_ref[...])
    acc_ref[...] = h
    hn_ref[...] = _rms_norm(h, fg_ref[...], NORM_EPS).astype(BF16)

    def conv(u, buf, tail, cw, cb, ci):
        buf[0:SUBLANES, :] = tail[ci]
        buf[SUBLANES:SUBLANES + tm, :] = u
        tail[ci] = u[tm - SUBLANES:, :]
        w = cw[ci]
        out = cb[ci] + w[hist:hist + 1, :] * u
        for d in range(1, hist + 1):
            out = out + w[hist - d:hist - d + 1, :] * buf[SUBLANES - d:SUBLANES - d + tm, :]
        return out

    def up(ci):
        hn = hn_ref[...]
        return _dot(hn, wg_ref[ci]), _dot(hn, wv_ref[ci])

    ahead = [up(ci) for ci in range(min(FF_LOOKAHEAD, n_chunks))]
    for ci in range(n_chunks):
        u_g, u_v = ahead.pop(0)
        if ci + FF_LOOKAHEAD < n_chunks:
            ahead.append(up(ci + FF_LOOKAHEAD))
        gate = conv(u_g, buf_g.at[ci % 2], tail_g, cwg_ref, cbg_ref, ci)
        val = conv(u_v, buf_v.at[ci % 2], tail_v, cwv_ref, cbv_ref, ci)
        act = (gate * _sigmoid(gate)) * val
        act_ref[:, ci * FF_CHUNK:(ci + 1) * FF_CHUNK] = act.astype(BF16)
        if (ci + 1) % FF_DOWN_GROUP == 0 or ci + 1 == n_chunks:
            k0 = (ci // FF_DOWN_GROUP) * FF_DOWN_GROUP * FF_CHUNK
            k1 = (ci + 1) * FF_CHUNK
            acc_ref[...] += _dot(act_ref[:, k0:k1], wd_ref[k0:k1, :])

    h2 = acc_ref[...]
    gate = _sigmoid(_dot(_rms_norm(h2, pg_ref[...], NORM_EPS).astype(BF16), wpg_ref[...]))
    h3 = h2 + gate * _dot(p_ref[...].astype(BF16), wpp_ref[...])
    if final:
        h3 = _rms_norm(h3, fin_ref[...], NORM_EPS)
    o_ref[...] = h3


def _ffn_layer(h, p, layer, ffn_norm, w_up, conv_w, conv_b, w_down, ple_norm, ple_gate, ple_proj,
               mix=None, w_mix=None, final_norm=None):
    bsz, seq, d = h.shape
    d_ff = w_down.shape[0]
    n_conv = conv_w.shape[0]
    ple_dim = p.shape[-1]
    tm = min(ROW_TILE, seq)
    fc = FF_CHUNK
    n_chunks = d_ff // fc
    assert n_chunks * fc == d_ff

    def chunked_cols(w):
        return w.reshape(w.shape[0], n_chunks, fc).transpose(1, 0, 2)

    row = lambda b, i: (b, i, 0)
    args = [h]
    in_specs = [pl.BlockSpec((None, tm, d), row)]
    if mix is not None:
        args += [mix, w_mix.astype(BF16)]
        in_specs += [pl.BlockSpec((None, tm, mix.shape[-1]), row), _const_spec(w_mix.shape)]
    args += [
        p, ffn_norm.reshape(1, d),
        chunked_cols(w_up[:, :d_ff]).astype(BF16), chunked_cols(w_up[:, d_ff:]).astype(BF16),
        chunked_cols(conv_w[:, :d_ff]), chunked_cols(conv_w[:, d_ff:]),
        chunked_cols(conv_b[None, :d_ff]), chunked_cols(conv_b[None, d_ff:]),
        w_down.astype(BF16),
        ple_norm.reshape(1, d), ple_gate.astype(BF16), ple_proj.astype(BF16),
    ]
    in_specs += [
        pl.BlockSpec((None, None, tm, ple_dim), lambda b, i: (layer, b, i, 0)),
        _const_spec((1, d)),
        _const_spec((n_chunks, d, fc)), _const_spec((n_chunks, d, fc)),
        _const_spec((n_chunks, n_conv, fc)), _const_spec((n_chunks, n_conv, fc)),
        _const_spec((n_chunks, 1, fc)), _const_spec((n_chunks, 1, fc)),
        _const_spec((d_ff, d)),
        _const_spec((1, d)), _const_spec((d, d)), _const_spec((ple_dim, d)),
    ]
    if final_norm is not None:
        args.append(final_norm.reshape(1, d))
        in_specs.append(_const_spec((1, d)))
    return pl.pallas_call(
        functools.partial(_ffn_kernel, has_mix=mix is not None, final=final_norm is not None, n_conv=n_conv),
        grid=(bsz, seq // tm),
        in_specs=in_specs,
        out_specs=pl.BlockSpec((None, tm, d), row),
        out_shape=jax.ShapeDtypeStruct((bsz, seq, d), F32),
        scratch_shapes=[
            pltpu.VMEM((2, tm + SUBLANES, fc), F32), pltpu.VMEM((2, tm + SUBLANES, fc), F32),
            pltpu.VMEM((n_chunks, SUBLANES, fc), F32),
            pltpu.VMEM((n_chunks, SUBLANES, fc), F32),
            pltpu.VMEM((tm, d), F32),
            pltpu.VMEM((tm, d), BF16),
            pltpu.VMEM((tm, d_ff), BF16),
        ],
        compiler_params=_params("parallel", "arbitrary"),
        name="ffn_layer",
    )(*args)


def kernel(x, p, attn_norm, ffn_norm, ple_norm, ssm_in_proj, ssm_conv_w, ssm_conv_b, ssm_dt_bias, ssm_a_log, ssm_d, ssm_norm, ssm_out_proj, kv_norm, w_kv, w_q, w_o, ffn_up, ffn_conv_w, ffn_conv_b, ffn_down, ple_gate, ple_proj, final_norm):
    depth = attn_norm.shape[0]
    n_a = ssm_in_proj.shape[0]
    d_inner = ssm_out_proj.shape[1]
    conv_dim = ssm_conv_w.shape[-1]
    n_heads = ssm_dt_bias.shape[-1]
    sb_width = w_q.shape[-1]
    q_scale = (SB_HEAD_DIM ** -0.5) * LOG2E

    h = x
    k = v_t = None
    for i in range(depth):
        ffn_args = (ffn_norm[i], ffn_up[i], ffn_conv_w[i], ffn_conv_b[i], ffn_down[i],
                    ple_norm[i], ple_gate[i], ple_proj[i])
        fin = final_norm if i == depth - 1 else None
        if i < n_a:
            w_in = ssm_in_proj[i]
            w_z = w_in[:, :d_inner].astype(BF16)
            w_xbc = w_in[:, d_inner:d_inner + conv_dim].astype(BF16)
            w_dt = jnp.pad(w_in[:, d_inner + conv_dim:], ((0, 0), (0, LANES - n_heads))).astype(BF16)
            z, xc, dt = _ssm_in_proj(h, attn_norm[i], w_z, w_xbc, w_dt, ssm_conv_w[i], ssm_conv_b[i])
            h = _ssd_layer(h, z, xc, dt, ssm_dt_bias[i], ssm_a_log[i], ssm_d[i], ssm_norm[i], ssm_out_proj[i])
            h = _ffn_layer(h, p, i, *ffn_args, final_norm=fin)
        else:
            j = i - n_a
            (q,) = _norm_proj(h, attn_norm[i], [(w_q[j] * q_scale).astype(BF16)], [False], [BF16])
            o = _sb_attention(q, k, v_t)
            h = _ffn_layer(h, p, i, *ffn_args, mix=o, w_mix=w_o[j], final_norm=fin)
        if i == n_a - 1:
            w_k = w_kv[:, :sb_width].astype(BF16)
            w_v_t = w_kv[:, sb_width:].T.astype(BF16)
            k, v_t = _norm_proj(h, kv_norm, [w_k, w_v_t], [False, True], [BF16, BF16])
    return h
```

```python
import functools
import math

import jax
import jax.numpy as jnp
from jax import lax
from jax.experimental import pallas as pl
from jax.experimental.pallas import tpu as pltpu

F32 = jnp.float32
BF16 = jnp.bfloat16

NORM_EPS = 1e-6
SSM_NORM_EPS = 1e-5
SSM_HEAD_DIM = 64
SSM_GROUPS = 4
SSM_STATE = 128
SSM_CHUNK = 128
SSD_BATCH = 4
SB_HEAD_DIM = 64
LANES = 128
SUBLANES = 8
SB_TILE = 256
SB_STEP_HEADS = 16
SB_UNDERFLOW_LOG2 = 160.0
ROW_TILE = 512
FF_CHUNK = 256
FF_LOOKAHEAD = 2
FF_DOWN_GROUP = 11
SSM_SLAB = 256
VMEM_LIMIT = 56 * 1024 * 1024
LOG2E = 1.4426950408889634


def _params(*sem):
    return pltpu.CompilerParams(dimension_semantics=sem, vmem_limit_bytes=VMEM_LIMIT)


def _const_spec(shape):
    nd = len(shape)
    return pl.BlockSpec(shape, lambda *_: (0,) * nd, pipeline_mode=pl.Buffered(1))


def _rms_norm(x, gain, eps):
    return x * lax.rsqrt(jnp.mean(x * x, axis=-1, keepdims=True) + eps) * gain


def _sigmoid(x):
    return 1.0 / (1.0 + jnp.exp(-x))


def _softplus(x):
    return jnp.maximum(x, 0.0) + jnp.log1p(jnp.exp(-jnp.abs(x)))


def _dot(a, b):
    return jnp.dot(a, b, preferred_element_type=F32)


def _dot_nt(a, b):
    return lax.dot_general(a, b, (((1,), (1,)), ((), ())), preferred_element_type=F32)


def _split3(x):
    hi = x.astype(BF16)
    r1 = x - hi.astype(F32)
    mid = r1.astype(BF16)
    lo = (r1 - mid.astype(F32)).astype(BF16)
    return hi, mid, lo


def _norm_proj_kernel(h_ref, g_ref, *refs, transposed, eps):
    n = len(transposed)
    w_refs, o_refs = refs[:n], refs[n:]
    hn = _rms_norm(h_ref[...], g_ref[...], eps).astype(BF16)
    for w_ref, o_ref, tr in zip(w_refs, o_refs, transposed):
        if tr:
            o_ref[...] = _dot_nt(w_ref[...], hn).astype(o_ref.dtype)
        else:
            o_ref[...] = _dot(hn, w_ref[...]).astype(o_ref.dtype)


def _norm_proj(h, gain, weights, transposed, out_dtypes):
    bsz, seq, d = h.shape
    tm = min(ROW_TILE, seq)
    in_specs = [pl.BlockSpec((None, tm, d), lambda b, i: (b, i, 0)), _const_spec((1, d))]
    out_specs, out_shape = [], []
    for w, tr, dt in zip(weights, transposed, out_dtypes):
        in_specs.append(_const_spec(w.shape))
        if tr:
            n = w.shape[0]
            out_specs.append(pl.BlockSpec((None, n, tm), lambda b, i: (b, 0, i)))
            out_shape.append(jax.ShapeDtypeStruct((bsz, n, seq), dt))
        else:
            n = w.shape[1]
            out_specs.append(pl.BlockSpec((None, tm, n), lambda b, i: (b, i, 0)))
            out_shape.append(jax.ShapeDtypeStruct((bsz, seq, n), dt))
    return pl.pallas_call(
        functools.partial(_norm_proj_kernel, transposed=tuple(transposed), eps=NORM_EPS),
        grid=(bsz, seq // tm),
        in_specs=in_specs, out_specs=out_specs, out_shape=out_shape,
        compiler_params=_params("parallel", "parallel"),
        name="norm_proj",
    )(h, gain.reshape(1, d), *weights)


def _ssm_in_kernel(h_ref, g_ref, wz_ref, wx_ref, wdt_ref, cw_ref, cb_ref, z_ref, xc_ref, dt_ref,
                   hn_ref, buf, tail, *, n_conv):
    tm = h_ref.shape[0]
    hist = n_conv - 1
    W = SSM_SLAB
    n_slabs = wx_ref.shape[1] // W
    n_z = wz_ref.shape[1] // W
    i = pl.program_id(1)

    @pl.when(i == 0)
    def _():
        tail[...] = jnp.zeros(tail.shape, F32)

    hn_ref[...] = _rms_norm(h_ref[...], g_ref[...], NORM_EPS).astype(BF16)

    def up(s):
        return _dot(hn_ref[...], wx_ref[:, s * W:(s + 1) * W])

    ahead = [up(s) for s in range(min(FF_LOOKAHEAD, n_slabs))]
    for s in range(n_slabs):
        cols = slice(s * W, (s + 1) * W)
        u = ahead.pop(0)
        if s + FF_LOOKAHEAD < n_slabs:
            ahead.append(up(s + FF_LOOKAHEAD))
        if s < n_z:
            z_ref[:, cols] = _dot(hn_ref[...], wz_ref[:, cols])
        win = buf.at[s % 2]
        win[0:SUBLANES, :] = tail[s]
        win[SUBLANES:SUBLANES + tm, :] = u
        tail[s] = u[tm - SUBLANES:, :]
        acc = cb_ref[:, cols] + cw_ref[hist:hist + 1, cols] * u
        for d in range(1, hist + 1):
            acc = acc + cw_ref[hist - d:hist - d + 1, cols] * win[SUBLANES - d:SUBLANES - d + tm, :]
        xc_ref[:, cols] = acc * _sigmoid(acc)
    for s in range(n_slabs, n_z):
        z_ref[:, s * W:(s + 1) * W] = _dot(hn_ref[...], wz_ref[:, s * W:(s + 1) * W])
    dt_ref[...] = _dot(hn_ref[...], wdt_ref[...])


def _ssm_in_proj(h, gain, w_z, w_xbc, w_dt, conv_w, conv_b):
    bsz, seq, d = h.shape
    tm = min(ROW_TILE, seq)
    d_inner, conv_dim, n_conv = w_z.shape[1], w_xbc.shape[1], conv_w.shape[0]
    assert conv_dim % SSM_SLAB == 0 and d_inner % SSM_SLAB == 0
    row = lambda b, i: (b, i, 0)
    return pl.pallas_call(
        functools.partial(_ssm_in_kernel, n_conv=n_conv),
        grid=(bsz, seq // tm),
        in_specs=[
            pl.BlockSpec((None, tm, d), row), _const_spec((1, d)),
            _const_spec(w_z.shape), _const_spec(w_xbc.shape), _const_spec(w_dt.shape),
            _const_spec((n_conv, conv_dim)), _const_spec((1, conv_dim)),
        ],
        out_specs=[pl.BlockSpec((None, tm, d_inner), row), pl.BlockSpec((None, tm, conv_dim), row),
                   pl.BlockSpec((None, tm, LANES), row)],
        out_shape=[jax.ShapeDtypeStruct((bsz, seq, d_inner), F32), jax.ShapeDtypeStruct((bsz, seq, conv_dim), F32),
                   jax.ShapeDtypeStruct((bsz, seq, LANES), F32)],
        scratch_shapes=[
            pltpu.VMEM((tm, d), BF16),
            pltpu.VMEM((2, tm + SUBLANES, SSM_SLAB), F32),
            pltpu.VMEM((conv_dim // SSM_SLAB, SUBLANES, SSM_SLAB), F32),
        ],
        compiler_params=_params("parallel", "arbitrary"),
        name="ssm_in_proj",
    )(h, gain.reshape(1, d), w_z, w_xbc, w_dt, conv_w, conv_b.reshape(1, conv_dim))


def _ssd_kernel(z_ref, xc, dt_ref, h_ref, dtb_ref, alog_ref, dexp_ref, nw_ref,
                wo_ref, o_ref, state, ybuf, *, d_inner):
    L, N, G = SSM_CHUNK, SSM_STATE, SSM_GROUPS
    gw = d_inner // G
    pairs_per_group = gw // LANES
    batches = range(z_ref.shape[0])
    c = pl.program_id(1)

    @pl.when(c == 0)
    def _():
        state[...] = jnp.zeros(state.shape, F32)

    row = lax.broadcasted_iota(jnp.int32, (L, L), 0)
    col = lax.broadcasted_iota(jnp.int32, (L, L), 1)
    tri = col <= row
    tri_b = jnp.where(tri, 1.0, 0.0).astype(BF16)
    lane = lax.broadcasted_iota(jnp.int32, (L, LANES), 1)
    first_head = lane < SSM_HEAD_DIM
    neg_inf = jnp.float32(-jnp.inf)
    neg_a = -jnp.exp(alog_ref[...])

    a_cs, a_cs_t, dt_t, e_cs, w_t, a_last = [], [], [], [], [], []
    for b in batches:
        dt = _softplus(dt_ref[b] + dtb_ref[...])
        cs = sum(_dot(tri_b, t) for t in _split3(dt * neg_a))
        a_cs.append(cs)
        a_cs_t.append(cs.T)
        dt_t.append(dt.T)
        e_cs.append(jnp.exp(cs))
        w_t.append(jnp.exp(a_cs_t[b][:, L - 1:L] - a_cs_t[b]) * dt_t[b])
        a_last.append(cs[L - 1:L, :])

    for g in range(G):
        c_g, cb, b_g_t = [], [], []
        for b in batches:
            b_g = xc[b, :, d_inner + g * N:d_inner + (g + 1) * N]
            c_g.append(xc[b, :, d_inner + G * N + g * N:d_inner + G * N + (g + 1) * N])
            cb.append(_dot_nt(c_g[b].astype(BF16), b_g.astype(BF16)))
            b_g_t.append(b_g.T)
        for j in range(pairs_per_group):
            h0 = (g * pairs_per_group + j) * 2
            cols = slice((g * pairs_per_group + j) * LANES, (g * pairs_per_group + j + 1) * LANES)
            for b in batches:
                x_pair = xc[b, :, cols].astype(BF16)
                s_pair = state[b, g, :, j * LANES:(j + 1) * LANES]
                rhs = jnp.concatenate([x_pair, s_pair.astype(BF16)], axis=0)
                ys, sts = [], []
                for hh in (h0, h0 + 1):
                    diff = a_cs[b][:, hh:hh + 1] - a_cs_t[b][hh:hh + 1, :]
                    l_h = cb[b] * jnp.exp(jnp.where(tri, diff, neg_inf)) * dt_t[b][hh:hh + 1, :]
                    c_h = c_g[b] * e_cs[b][:, hh:hh + 1]
                    lhs = jnp.concatenate([l_h.astype(BF16), c_h.astype(BF16)], axis=1)
                    ys.append(_dot(lhs, rhs))
                    sts.append(_dot((b_g_t[b] * w_t[b][hh:hh + 1, :]).astype(BF16), x_pair))
                ybuf[b, :, cols] = jnp.where(first_head, ys[0], ys[1])
                d_a = jnp.where(first_head[0:1, :], jnp.exp(a_last[b][:, h0:h0 + 1]),
                                jnp.exp(a_last[b][:, h0 + 1:h0 + 2]))
                state[b, g, :, j * LANES:(j + 1) * LANES] = s_pair * d_a + jnp.where(first_head, sts[0], sts[1])

    acc = None
    for g in range(G):
        sl = slice(g * gw, (g + 1) * gw)
        gns = []
        for b in batches:
            zz = z_ref[b, :, sl]
            gated = (ybuf[b, :, sl] + xc[b, :, sl] * dexp_ref[:, sl]) * (zz * _sigmoid(zz))
            gn = gated * lax.rsqrt(jnp.mean(gated * gated, axis=-1, keepdims=True) + SSM_NORM_EPS) * nw_ref[:, sl]
            gns.append(gn.astype(BF16))
        part = _dot(jnp.concatenate(gns, axis=0), wo_ref[sl, :])
        acc = part if acc is None else acc + part
    for b in batches:
        o_ref[b] = h_ref[b] + acc[b * L:(b + 1) * L, :]


def _ssd_layer(h, z, xc, dt, dt_bias, a_log, d_skip, norm_w, out_proj):
    bsz, seq, d = h.shape
    d_inner = z.shape[-1]
    conv_dim = xc.shape[-1]
    L = SSM_CHUNK
    nb = SSD_BATCH if bsz % SSD_BATCH == 0 else 1
    pad = LANES - dt_bias.shape[0]
    row = lambda b, c: (b, c, 0)
    return pl.pallas_call(
        functools.partial(_ssd_kernel, d_inner=d_inner),
        grid=(bsz // nb, seq // L),
        in_specs=[
            pl.BlockSpec((nb, L, d_inner), row),
            pl.BlockSpec((nb, L, conv_dim), row),
            pl.BlockSpec((nb, L, LANES), row),
            pl.BlockSpec((nb, L, d), row),
            _const_spec((1, LANES)), _const_spec((1, LANES)),
            _const_spec((1, d_inner)), _const_spec((1, d_inner)),
            _const_spec((d_inner, d)),
        ],
        out_specs=pl.BlockSpec((nb, L, d), row),
        out_shape=jax.ShapeDtypeStruct((bsz, seq, d), F32),
        scratch_shapes=[
            pltpu.VMEM((nb, SSM_GROUPS, SSM_STATE, d_inner // SSM_GROUPS), F32),
            pltpu.VMEM((nb, L, d_inner), F32),
        ],
        compiler_params=_params("parallel", "arbitrary"),
        name="ssd_layer",
    )(z, xc, dt, h,
      jnp.pad(dt_bias, (0, pad)).reshape(1, LANES), jnp.pad(a_log, (0, pad)).reshape(1, LANES),
      jnp.repeat(d_skip, SSM_HEAD_DIM).reshape(1, d_inner), norm_w.reshape(1, d_inner),
      out_proj.astype(BF16))


def _sb_attn_kernel(q_ref, k_ref, vt_ref, o_ref, run_ref, acc_ref):
    T, DH = SB_TILE, SB_HEAD_DIM
    n_heads = q_ref.shape[1] // DH
    qi = pl.program_id(2)
    lane = lax.broadcasted_iota(jnp.int32, (T, LANES), 1)
    q_heads = []
    for p in range(n_heads // 2):
        q2 = q_ref[:, p * LANES:(p + 1) * LANES]
        q_heads += [jnp.where(lane < DH, q2, jnp.zeros_like(q2)), jnp.where(lane < DH, jnp.zeros_like(q2), q2)]
    row = lax.broadcasted_iota(jnp.int32, (T, T), 0)
    col = lax.broadcasted_iota(jnp.int32, (T, T), 1)
    tri_b = jnp.where(col >= row, 1.0, 0.0).astype(BF16)
    causal = row < col
    heads = range(n_heads)

    def block(tiles):
        starts = [pl.multiple_of(kb * T, T) for kb, _ in tiles]
        streams = [(i, h) for i in range(len(tiles)) for h in heads]
        zs = {(i, h): _dot_nt(k_ref[pl.ds(starts[i], T), (h // 2) * LANES:(h // 2 + 1) * LANES], q_heads[h])
              for i, h in streams}
        sps = {}
        for i, h in streams:
            z = zs[i, h]
            sp = jnp.maximum(jnp.log2(1.0 + jnp.exp2(jnp.minimum(z, 40.0))), z)
            if tiles[i][1]:
                sp = jnp.where(causal, sp, 0.0)
            sps[i, h] = sp.astype(BF16)
        rests = {s: _dot(tri_b, sps[s]) for s in streams}
        ws = {}
        for i, h in streams:
            w = jnp.exp2(zs[i, h] - rests[i, h] - run_ref[h:h + 1, :])
            if tiles[i][1]:
                w = jnp.where(causal, w, 0.0)
            ws[i, h] = w.astype(BF16)
            run_ref[h:h + 1, :] += rests[i, h][0:1, :]
        for i, h in streams:
            acc_ref[h * DH:(h + 1) * DH, :] += _dot(vt_ref[h * DH:(h + 1) * DH, pl.ds(starts[i], T)], ws[i, h])

    run_ref[...] = jnp.zeros(run_ref.shape, F32)
    acc_ref[...] = jnp.zeros(acc_ref.shape, F32)

    @pl.when(qi == 0)
    def _():
        block([(qi, True)])

    @pl.when(qi > 0)
    def _():
        block([(qi, True), (qi - 1, False)])

    def all_underflowed():
        return jnp.min(run_ref[0:n_heads, :]) >= SB_UNDERFLOW_LOG2

    def cond(c):
        t, done = c
        return jnp.logical_and(t < qi - 1, jnp.logical_not(done))

    def body(c):
        t, _ = c
        block([(qi - 2 - t, False)])
        return t + 1, all_underflowed()

    lax.while_loop(cond, body, (jnp.int32(0), all_underflowed()))
    o_ref[...] = acc_ref[...].T.astype(o_ref.dtype)


def _sb_attention(q, k, v_t):
    bsz, seq, width = q.shape
    T = SB_TILE
    W = SB_STEP_HEADS * SB_HEAD_DIM
    return pl.pallas_call(
        _sb_attn_kernel,
        grid=(bsz, width // W, seq // T),
        in_specs=[
            pl.BlockSpec((None, T, W), lambda b, p, i: (b, i, p)),
            pl.BlockSpec((None, seq, W), lambda b, p, i: (b, 0, p)),
            pl.BlockSpec((None, W, seq), lambda b, p, i: (b, p, 0)),
        ],
        out_specs=pl.BlockSpec((None, T, W), lambda b, p, i: (b, i, p)),
        out_shape=jax.ShapeDtypeStruct((bsz, seq, width), BF16),
        scratch_shapes=[
            pltpu.VMEM((max(SB_STEP_HEADS, SUBLANES), T), F32),
            pltpu.VMEM((W, T), F32),
        ],
        compiler_params=_params("parallel", "parallel", "arbitrary"),
        name="sb_attention",
    )(q, k, v_t)


def _ffn_kernel(*refs, has_mix, final, n_conv):
    refs = list(refs)
    h_ref = refs.pop(0)
    if has_mix:
        mix_ref, wmix_ref = refs.pop(0), refs.pop(0)
    (p_ref, fg_ref, wg_ref, wv_ref, cwg_ref, cwv_ref, cbg_ref, cbv_ref, wd_ref,
     pg_ref, wpg_ref, wpp_ref) = refs[:12]
    refs = refs[12:]
    if final:
        fin_ref = refs.pop(0)
    o_ref, buf_g, buf_v, tail_g, tail_v, acc_ref, hn_ref, act_ref = refs
    tm = h_ref.shape[0]
    n_chunks = wg_ref.shape[0]
    hist = n_conv - 1
    i = pl.program_id(1)

    @pl.when(i == 0)
    def _():
        tail_g[...] = jnp.zeros(tail_g.shape, F32)
        tail_v[...] = jnp.zeros(tail_v.shape, F32)

    h = h_ref[...]
    if has_mix:
        h = h + _dot(mix_ref[...], wmix_ref[...])
    acc_ref[...] = h
    hn_ref[...] = _rms_norm(h, fg_ref[...], NORM_EPS).astype(BF16)

    def conv(u, buf, tail, cw, cb, ci):
        buf[0:SUBLANES, :] = tail[ci]
        buf[SUBLANES:SUBLANES + tm, :] = u
        tail[ci] = u[tm - SUBLANES:, :]
        w = cw[ci]
        out = cb[ci] + w[hist:hist + 1, :] * u
        for d in range(1, hist + 1):
            out = out + w[hist - d:hist - d + 1, :] * buf[SUBLANES - d:SUBLANES - d + tm, :]
        return out

    def up(ci):
        hn = hn_ref[...]
        return _dot(hn, wg_ref[ci]), _dot(hn, wv_ref[ci])

    ahead = [up(ci) for ci in range(min(FF_LOOKAHEAD, n_chunks))]
    for ci in range(n_chunks):
        u_g, u_v = ahead.pop(0)
        if ci + FF_LOOKAHEAD < n_chunks:
            ahead.append(up(ci + FF_LOOKAHEAD))
        gate = conv(u_g, buf_g.at[ci % 2], tail_g, cwg_ref, cbg_ref, ci)
        val = conv(u_v, buf_v.at[ci % 2], tail_v, cwv_ref, cbv_ref, ci)
        act = (gate * _sigmoid(gate)) * val
        act_ref[:, ci * FF_CHUNK:(ci + 1) * FF_CHUNK] = act.astype(BF16)
        if (ci + 1) % FF_DOWN_GROUP == 0 or ci + 1 == n_chunks:
            k0 = (ci // FF_DOWN_GROUP) * FF_DOWN_GROUP * FF_CHUNK
            k1 = (ci + 1) * FF_CHUNK
            acc_ref[...] += _dot(act_ref[:, k0:k1], wd_ref[k0:k1, :])

    h2 = acc_ref[...]
    gate = _sigmoid(_dot(_rms_norm(h2, pg_ref[...], NORM_EPS).astype(BF16), wpg_ref[...]))
    h3 = h2 + gate * _dot(p_ref[...].astype(BF16), wpp_ref[...])
    if final:
        h3 = _rms_norm(h3, fin_ref[...], NORM_EPS)
    o_ref[...] = h3


def _ffn_layer(h, p, layer, ffn_norm, w_up, conv_w, conv_b, w_down, ple_norm, ple_gate, ple_proj,
               mix=None, w_mix=None, final_norm=None):
    bsz, seq, d = h.shape
    d_ff = w_down.shape[0]
    n_conv = conv_w.shape[0]
    ple_dim = p.shape[-1]
    tm = min(ROW_TILE, seq)
    fc = FF_CHUNK
    n_chunks = d_ff // fc
    assert n_chunks * fc == d_ff

    def chunked_cols(w):
        return w.reshape(w.shape[0], n_chunks, fc).transpose(1, 0, 2)

    row = lambda b, i: (b, i, 0)
    args = [h]
    in_specs = [pl.BlockSpec((None, tm, d), row)]
    if mix is not None:
        args += [mix, w_mix.astype(BF16)]
        in_specs += [pl.BlockSpec((None, tm, mix.shape[-1]), row), _const_spec(w_mix.shape)]
    args += [
        p, ffn_norm.reshape(1, d),
        chunked_cols(w_up[:, :d_ff]).astype(BF16), chunked_cols(w_up[:, d_ff:]).astype(BF16),
        chunked_cols(conv_w[:, :d_ff]), chunked_cols(conv_w[:, d_ff:]),
        chunked_cols(conv_b[None, :d_ff]), chunked_cols(conv_b[None, d_ff:]),
        w_down.astype(BF16),
        ple_norm.reshape(1, d), ple_gate.astype(BF16), ple_proj.astype(BF16),
    ]
    in_specs += [
        pl.BlockSpec((None, None, tm, ple_dim), lambda b, i: (layer, b, i, 0)),
        _const_spec((1, d)),
        _const_spec((n_chunks, d, fc)), _const_spec((n_chunks, d, fc)),
        _const_spec((n_chunks, n_conv, fc)), _const_spec((n_chunks, n_conv, fc)),
        _const_spec((n_chunks, 1, fc)), _const_spec((n_chunks, 1, fc)),
        _const_spec((d_ff, d)),
        _const_spec((1, d)), _const_spec((d, d)), _const_spec((ple_dim, d)),
    ]
    if final_norm is not None:
        args.append(final_norm.reshape(1, d))
        in_specs.append(_const_spec((1, d)))
    return pl.pallas_call(
        functools.partial(_ffn_kernel, has_mix=mix is not None, final=final_norm is not None, n_conv=n_conv),
        grid=(bsz, seq // tm),
        in_specs=in_specs,
        out_specs=pl.BlockSpec((None, tm, d), row),
        out_shape=jax.ShapeDtypeStruct((bsz, seq, d), F32),
        scratch_shapes=[
            pltpu.VMEM((2, tm + SUBLANES, fc), F32), pltpu.VMEM((2, tm + SUBLANES, fc), F32),
            pltpu.VMEM((n_chunks, SUBLANES, fc), F32),
            pltpu.VMEM((n_chunks, SUBLANES, fc), F32),
            pltpu.VMEM((tm, d), F32),
            pltpu.VMEM((tm, d), BF16),
            pltpu.VMEM((tm, d_ff), BF16),
        ],
        compiler_params=_params("parallel", "arbitrary"),
        name="ffn_layer",
    )(*args)


def kernel(x, p, attn_norm, ffn_norm, ple_norm, ssm_in_proj, ssm_conv_w, ssm_conv_b, ssm_dt_bias, ssm_a_log, ssm_d, ssm_norm, ssm_out_proj, kv_norm, w_kv, w_q, w_o, ffn_up, ffn_conv_w, ffn_conv_b, ffn_down, ple_gate, ple_proj, final_norm):
    depth = attn_norm.shape[0]
    n_a = ssm_in_proj.shape[0]
    d_inner = ssm_out_proj.shape[1]
    conv_dim = ssm_conv_w.shape[-1]
    n_heads = ssm_dt_bias.shape[-1]
    sb_width = w_q.shape[-1]
    q_scale = (SB_HEAD_DIM ** -0.5) * LOG2E

    h = x
    k = v_t = None
    for i in range(depth):
        ffn_args = (ffn_norm[i], ffn_up[i], ffn_conv_w[i], ffn_conv_b[i], ffn_down[i],
                    ple_norm[i], ple_gate[i], ple_proj[i])
        fin = final_norm if i == depth - 1 else None
        if i < n_a:
            w_in = ssm_in_proj[i]
            w_z = w_in[:, :d_inner].astype(BF16)
            w_xbc = w_in[:, d_inner:d_inner + conv_dim].astype(BF16)
            w_dt = jnp.pad(w_in[:, d_inner + conv_dim:], ((0, 0), (0, LANES - n_heads))).astype(BF16)
            z, xc, dt = _ssm_in_proj(h, attn_norm[i], w_z, w_xbc, w_dt, ssm_conv_w[i], ssm_conv_b[i])
            h = _ssd_layer(h, z, xc, dt, ssm_dt_bias[i], ssm_a_log[i], ssm_d[i], ssm_norm[i], ssm_out_proj[i])
            h = _ffn_layer(h, p, i, *ffn_args, final_norm=fin)
        else:
            j = i - n_a
            (q,) = _norm_proj(h, attn_norm[i], [(w_q[j] * q_scale).astype(BF16)], [False], [BF16])
            o = _sb_attention(q, k, v_t)
            h = _ffn_layer(h, p, i, *ffn_args, mix=o, w_mix=w_o[j], final_norm=fin)
        if i == n_a - 1:
            w_k = w_kv[:, :sb_width].astype(BF16)
            w_v_t = w_kv[:, sb_width:].T.astype(BF16)
            k, v_t = _norm_proj(h, kv_norm, [w_k, w_v_t], [False, True], [BF16, BF16])
    return h
```
